```python
import math
import jax, jax.numpy as jnp
from jax import lax
import numpy as np

D_MODEL = 1024
BATCH = 4
SEQ = 8192
DEPTH = 2

ATT_HEADS = 4
ATT_HEAD_DIM = 64
ATT_V_DIM = 2 * ATT_HEAD_DIM
ATT_QK = ATT_HEADS * 2 * ATT_HEAD_DIM
ATT_WIDTH = ATT_HEADS * ATT_V_DIM
ROPE_THETA = 500000.0
ROPE_DIM = ATT_HEAD_DIM // 4
Q_BLOCK = 128

SSM_HEAD_DIM = 64
SSM_WIDTH = D_MODEL
SSM_HEADS = SSM_WIDTH // SSM_HEAD_DIM
SSM_GROUPS = 2
SSM_STATE = 128
SSM_CONV = 4
SSM_CHUNK = 128
CONV_CH = SSM_WIDTH + 2 * SSM_GROUPS * SSM_STATE

EPS = 1e-5

_SIZES = [ATT_QK, ATT_QK, ATT_WIDTH, ATT_WIDTH,
          SSM_WIDTH, CONV_CH, SSM_HEADS,
          D_MODEL, D_MODEL]
IN_COLS = sum(_SIZES)
_OFFS = [int(v) for v in np.cumsum([0] + _SIZES)]

kernel_name = "hybrid_diffattn_ssd_gated_merge"


def rms_norm(x, w):
    xf = x.astype(jnp.float32)
    y = xf * lax.rsqrt(jnp.mean(xf * xf, axis=-1, keepdims=True) + EPS)
    return (y * w.astype(jnp.float32)).astype(x.dtype)


def rope_partial(t, cos, sin):
    half = ROPE_DIM // 2
    tr = t[..., :ROPE_DIM].astype(jnp.float32)
    t1, t2 = tr[..., :half], tr[..., half:]
    cc = cos[:, :, None, None, :]
    ss = sin[:, :, None, None, :]
    rot = jnp.concatenate([t1 * cc - t2 * ss, t2 * cc + t1 * ss], axis=-1)
    return jnp.concatenate([rot.astype(t.dtype), t[..., ROPE_DIM:]], axis=-1)


def diff_attention(q, k, v, lam):
    bsz, seq = q.shape[0], q.shape[1]
    nb = seq // Q_BLOCK
    scale = ATT_HEAD_DIM ** -0.5
    kf = k.astype(jnp.float32)
    vf = v.astype(jnp.float32)
    lamf = lam.astype(jnp.float32)
    qb = q.astype(jnp.float32).reshape(bsz, nb, Q_BLOCK, ATT_HEADS, 2, ATT_HEAD_DIM)
    qb = qb.transpose(1, 0, 2, 3, 4, 5)
    key_idx = jnp.arange(seq)

    def block(args):
        qi, bi = args
        s = jnp.einsum('bqhtd,bkhtd->bhtqk', qi, kf) * scale
        q_idx = bi * Q_BLOCK + jnp.arange(Q_BLOCK)
        mask = key_idx[None, :] <= q_idx[:, None]
        s = jnp.where(mask, s, -jnp.inf)
        p = jax.nn.softmax(s, axis=-1)
        a = p[:, :, 0] - lamf * p[:, :, 1]
        return jnp.einsum('bhqk,bkhe->bqhe', a, vf)

    out = lax.map(block, (qb, jnp.arange(nb)))
    return out.transpose(1, 0, 2, 3, 4).reshape(bsz, seq, ATT_HEADS, ATT_V_DIM)


def causal_dwconv(u, w, b):
    out = lax.conv_general_dilated(
        u, w[:, None, :], window_strides=(1,), padding=[(SSM_CONV - 1, 0)],
        dimension_numbers=('NWC', 'WIO', 'NWC'), feature_group_count=u.shape[-1])
    return out + b


def ssd_chunked(x, dt, a, b_mat, c_mat):
    bsz, seq = x.shape[0], x.shape[1]
    L = SSM_CHUNK
    nc = seq // L
    G, J, P, N = SSM_GROUPS, SSM_HEADS // SSM_GROUPS, SSM_HEAD_DIM, SSM_STATE
    xd = (x.astype(jnp.float32) * dt[..., None]).reshape(bsz, nc, L, G, J, P)
    adt = (dt * a).reshape(bsz, nc, L, G, J).transpose(0, 3, 4, 1, 2)
    acs = jnp.cumsum(adt, axis=-1)
    bc = b_mat.astype(jnp.float32).reshape(bsz, nc, L, G, N)
    cc = c_mat.astype(jnp.float32).reshape(bsz, nc, L, G, N)
    causal = jnp.tril(jnp.ones((L, L), dtype=bool))
    seg = acs[..., :, None] - acs[..., None, :]
    lmat = jnp.exp(jnp.where(causal, seg, -jnp.inf))
    cb = jnp.einsum('bclgn,bcsgn->bgcls', cc, bc)
    y_diag = jnp.einsum('bgcls,bgjcls,bcsgjp->bclgjp', cb, lmat, xd)
    decay = jnp.exp(acs[..., -1:] - acs)
    states = jnp.einsum('bclgn,bgjcl,bclgjp->bcgjpn', bc, decay, xd)
    chunk_decay = jnp.exp(acs[..., -1])

    def step(h, inp):
        st, dec = inp
        return h * dec[..., None, None] + st, h

    h0 = jnp.zeros((bsz, G, J, P, N), jnp.float32)
    _, prev = lax.scan(step, h0, (states.transpose(1, 0, 2, 3, 4, 5),
                                  chunk_decay.transpose(3, 0, 1, 2)))
    prev = prev.transpose(1, 0, 2, 3, 4, 5)
    y_off = jnp.einsum('bclgn,bcgjpn,bgjcl->bclgjp', cc, prev, jnp.exp(acs))
    return (y_diag + y_off).reshape(bsz, seq, SSM_HEADS, P)


def hybrid_layer(x, c, cos, sin, lambda_init, w_ada, b_ada, norm_w, w_in,
                 lambda_q1, lambda_k1, lambda_q2, lambda_k2, attn_subln_w, w_att_branch,
                 conv_w, conv_b, dt_bias, a_log, d_skip, ssm_norm_w, w_ssm_branch, w_out):
    bsz, seq = x.shape[0], x.shape[1]
    mod = c @ w_ada + b_ada
    shift, scale, gate = jnp.split(mod, 3, axis=-1)
    h = rms_norm(x, norm_w) * (1 + scale[:, None, :]) + shift[:, None, :]

    proj = h @ w_in
    q, k, v, att_g, z, xbc, dt_raw, mg_att, mg_ssm = [
        proj[..., _OFFS[i]:_OFFS[i + 1]] for i in range(len(_SIZES))]

    q = rope_partial(q.reshape(bsz, seq, ATT_HEADS, 2, ATT_HEAD_DIM), cos, sin)
    k = rope_partial(k.reshape(bsz, seq, ATT_HEADS, 2, ATT_HEAD_DIM), cos, sin)
    v = v.reshape(bsz, seq, ATT_HEADS, ATT_V_DIM)
    lam = (jnp.exp(jnp.sum(lambda_q1 * lambda_k1)) - jnp.exp(jnp.sum(lambda_q2 * lambda_k2))
           + lambda_init)
    o = diff_attention(q, k, v, lam)
    o = rms_norm(o, attn_subln_w) * (1.0 - lambda_init)
    o = o.reshape(bsz, seq, ATT_WIDTH).astype(x.dtype) * jax.nn.silu(att_g)
    y_att = o @ w_att_branch

    xbc = jax.nn.silu(causal_dwconv(xbc, conv_w, conv_b))
    xs = xbc[..., :SSM_WIDTH]
    bm = xbc[..., SSM_WIDTH:SSM_WIDTH + SSM_GROUPS * SSM_STATE].reshape(bsz, seq, SSM_GROUPS, SSM_STATE)
    cm = xbc[..., SSM_WIDTH + SSM_GROUPS * SSM_STATE:].reshape(bsz, seq, SSM_GROUPS, SSM_STATE)
    xs = xs.reshape(bsz, seq, SSM_HEADS, SSM_HEAD_DIM)
    dt = jax.nn.softplus(dt_raw.astype(jnp.float32) + dt_bias.astype(jnp.float32))
    a = -jnp.exp(a_log.astype(jnp.float32))
    y = ssd_chunked(xs, dt, a, bm, cm)
    y = y + d_skip.astype(jnp.float32)[:, None] * xs.astype(jnp.float32)
    y = y.reshape(bsz, seq, SSM_WIDTH) * jax.nn.silu(z.astype(jnp.float32))
    y = rms_norm(y.reshape(bsz, seq, SSM_GROUPS, SSM_WIDTH // SSM_GROUPS),
                 ssm_norm_w.reshape(SSM_GROUPS, SSM_WIDTH // SSM_GROUPS))
    y = y.reshape(bsz, seq, SSM_WIDTH).astype(x.dtype)
    y_ssm = y @ w_ssm_branch

    merged = jax.nn.sigmoid(mg_att) * y_att + jax.nn.sigmoid(mg_ssm) * y_ssm
    return x + gate[:, None, :] * (merged @ w_out)


def setup_inputs(seed: int = 0) -> dict:
    key = jax.random.key(seed)
    ks = jax.random.split(key, 24)
    f32 = jnp.float32

    def nrm(k, shape, s):
        return jax.random.normal(k, shape, f32) * s

    x = nrm(ks[0], (BATCH, SEQ, D_MODEL), 1.0)
    c = nrm(ks[1], (BATCH, D_MODEL), 1.0)
    offsets = jax.random.randint(ks[2], (BATCH, 1), 0, 4096, dtype=jnp.int32)
    positions = offsets + jnp.arange(SEQ, dtype=jnp.int32)[None, :]
    w_ada = nrm(ks[3], (DEPTH, D_MODEL, 3 * D_MODEL), 0.5 * D_MODEL ** -0.5)
    b_ada = nrm(ks[4], (DEPTH, 3 * D_MODEL), 0.02)
    norm_w = 1.0 + nrm(ks[5], (DEPTH, D_MODEL), 0.02)
    w_in = nrm(ks[6], (DEPTH, D_MODEL, IN_COLS), D_MODEL ** -0.5)
    lambda_q1 = nrm(ks[7], (DEPTH, ATT_HEAD_DIM), 0.1)
    lambda_k1 = nrm(ks[8], (DEPTH, ATT_HEAD_DIM), 0.1)
    lambda_q2 = nrm(ks[9], (DEPTH, ATT_HEAD_DIM), 0.1)
    lambda_k2 = nrm(ks[10], (DEPTH, ATT_HEAD_DIM), 0.1)
    attn_subln_w = 1.0 + nrm(ks[11], (DEPTH, ATT_V_DIM), 0.02)
    w_att_branch = nrm(ks[12], (DEPTH, ATT_WIDTH, D_MODEL), ATT_WIDTH ** -0.5)
    conv_w = nrm(ks[13], (DEPTH, SSM_CONV, CONV_CH), SSM_CONV ** -0.5)
    conv_b = nrm(ks[14], (DEPTH, CONV_CH), 0.02)
    dt0 = jnp.exp(jax.random.uniform(ks[15], (DEPTH, SSM_HEADS), f32,
                                     math.log(1e-3), math.log(1e-1)))
    dt_bias = dt0 + jnp.log(-jnp.expm1(-dt0))
    a_log = jnp.log(jax.random.uniform(ks[16], (DEPTH, SSM_HEADS), f32, 1.0, 16.0))
    d_skip = 1.0 + nrm(ks[17], (DEPTH, SSM_HEADS), 0.1)
    ssm_norm_w = 1.0 + nrm(ks[18], (DEPTH, SSM_WIDTH), 0.02)
    w_ssm_branch = nrm(ks[19], (DEPTH, SSM_WIDTH, D_MODEL), SSM_WIDTH ** -0.5)
    w_out = nrm(ks[20], (DEPTH, D_MODEL, D_MODEL), D_MODEL ** -0.5)
    final_norm_w = 1.0 + nrm(ks[21], (D_MODEL,), 0.02)
    return {"x": x, "c": c, "positions": positions, "w_ada": w_ada, "b_ada": b_ada,
            "norm_w": norm_w, "w_in": w_in, "lambda_q1": lambda_q1, "lambda_k1": lambda_k1,
            "lambda_q2": lambda_q2, "lambda_k2": lambda_k2, "attn_subln_w": attn_subln_w,
            "w_att_branch": w_att_branch, "conv_w": conv_w, "conv_b": conv_b,
            "dt_bias": dt_bias, "a_log": a_log, "d_skip": d_skip, "ssm_norm_w": ssm_norm_w,
            "w_ssm_branch": w_ssm_branch, "w_out": w_out, "final_norm_w": final_norm_w}


def reference(x, c, positions, w_ada, b_ada, norm_w, w_in, lambda_q1, lambda_k1,
              lambda_q2, lambda_k2, attn_subln_w, w_att_branch, conv_w, conv_b,
              dt_bias, a_log, d_skip, ssm_norm_w, w_ssm_branch, w_out, final_norm_w):
    inv_freq = ROPE_THETA ** (-jnp.arange(0, ROPE_DIM, 2, dtype=jnp.float32) / ROPE_DIM)
    ang = positions.astype(jnp.float32)[..., None] * inv_freq
    cos, sin = jnp.cos(ang), jnp.sin(ang)
    for l in range(DEPTH):
        lambda_init = 0.8 - 0.6 * math.exp(-0.3 * l)
        x = hybrid_layer(x, c, cos, sin, lambda_init, w_ada[l], b_ada[l], norm_w[l], w_in[l],
                         lambda_q1[l], lambda_k1[l], lambda_q2[l], lambda_k2[l],
                         attn_subln_w[l], w_att_branch[l], conv_w[l], conv_b[l],
                         dt_bias[l], a_log[l], d_skip[l], ssm_norm_w[l],
                         w_ssm_branch[l], w_out[l])
    return rms_norm(x, final_norm_w)
```

```python
import functools
import math

import numpy as np
import jax
import jax.numpy as jnp
from jax import lax
from jax.experimental import pallas as pl
from jax.experimental.pallas import tpu as pltpu

D_MODEL = 1024
DEPTH = 2
ATT_HEADS = 4
ATT_HEAD_DIM = 64
ATT_V_DIM = 2 * ATT_HEAD_DIM
ATT_WIDTH = ATT_HEADS * ATT_V_DIM
ROPE_THETA = 500000.0
ROPE_DIM = ATT_HEAD_DIM // 4
SSM_HEAD_DIM = 64
SSM_WIDTH = D_MODEL
SSM_HEADS = SSM_WIDTH // SSM_HEAD_DIM
SSM_GROUPS = 2
SSM_STATE = 128
SSM_CONV = 4
SSM_CHUNK = 128
GROUP_WIDTH = SSM_WIDTH // SSM_GROUPS
HEADS_PER_GROUP = SSM_HEADS // SSM_GROUPS
EPS = 1e-5

LANES = 128
SUBLANES = 8
CHUNK_COLS = 512
CH_Q, CH_K, CH_V, CH_G = 0, 1, 2, 3
CH_Z = 4
CH_XS = 6
CH_BC = 8
CH_MGA = 9
CH_MGS = 11
N_CHUNKS = 13
NEG_BIG = -1e30

VMEM_LIMIT = 48 * 1024 * 1024

TM_PROJ = 512
TQ = 512
TM_OUT = 512


def _cparams(sem):
    return pltpu.CompilerParams(dimension_semantics=sem, vmem_limit_bytes=VMEM_LIMIT)


def _sigmoid(x):
    return 1.0 / (1.0 + jnp.exp(-x))


def _silu(x):
    return x * _sigmoid(x)


def _split3(x):
    hi = x.astype(jnp.bfloat16)
    r1 = x - hi.astype(jnp.float32)
    mid = r1.astype(jnp.bfloat16)
    lo = (r1 - mid.astype(jnp.float32)).astype(jnp.bfloat16)
    return hi, mid, lo


def _ada_kernel(c_ref, w_ref, b_ref, o_ref):
    o_ref[...] = jnp.dot(c_ref[...], w_ref[...], preferred_element_type=jnp.float32,
                         precision=lax.Precision.HIGHEST) + b_ref[...]


def _ada_mod(c_pad, w_ada, b_ada):
    depth, d, d3 = w_ada.shape
    rows = c_pad.shape[0]
    nblk = d3 // d
    return pl.pallas_call(
        _ada_kernel,
        out_shape=jax.ShapeDtypeStruct((depth, rows, d3), jnp.float32),
        grid=(depth, nblk),
        in_specs=[pl.BlockSpec((rows, d), lambda l, n: (0, 0)),
                  pl.BlockSpec((None, d, d), lambda l, n: (l, 0, n)),
                  pl.BlockSpec((None, 1, d), lambda l, n: (l, 0, n))],
        out_specs=pl.BlockSpec((None, rows, d), lambda l, n: (l, 0, n)),
        compiler_params=_cparams(("arbitrary", "arbitrary")),
        name="ada_mod",
    )(c_pad, w_ada, b_ada.reshape(depth, 1, d3))


def _rope_kernel(pos_ref, freq_ref, ma_ref, mb_ref, c_ref, sa_ref, sb_ref):
    ang = pos_ref[...].astype(jnp.float32) * freq_ref[...]
    sn = jnp.sin(ang)
    c_ref[...] = jnp.cos(ang)
    sa_ref[...] = sn * ma_ref[...]
    sb_ref[...] = sn * mb_ref[...]


def _rope_tables(positions):
    t = positions.size
    half = ROPE_DIM // 2
    inv_freq = ROPE_THETA ** (-jnp.arange(0, ROPE_DIM, 2, dtype=jnp.float32) / ROPE_DIM)
    d = np.arange(LANES) % ATT_HEAD_DIM
    in_rope = d < ROPE_DIM
    freq = jnp.where(in_rope, inv_freq[d % half], 0.0).astype(jnp.float32).reshape(1, LANES)
    ma = jnp.asarray((in_rope & (d >= half)).astype(np.float32)).reshape(1, LANES)
    mb = jnp.asarray(-(d < half).astype(np.float32)).reshape(1, LANES)
    pos_b = jnp.broadcast_to(positions.reshape(t, 1), (t, LANES))
    tr = 1024
    row = pl.BlockSpec((1, LANES), lambda i: (0, 0))
    tab = pl.BlockSpec((tr, LANES), lambda i: (i, 0))
    sds = jax.ShapeDtypeStruct((t, LANES), jnp.float32)
    return pl.pallas_call(
        _rope_kernel, out_shape=(sds, sds, sds), grid=(t // tr,),
        in_specs=[tab, row, row, row], out_specs=(tab, tab, tab),
        compiler_params=_cparams(("arbitrary",)), name="rope_tables",
    )(pos_b, freq, ma, mb)


def _inproj_kernel(x_ref, nw_ref, sc_ref, sh_ref, w_ref, wdt_ref, c_ref, sa_ref, sb_ref,
                   p_ref, dt_ref, h_scr):
    n = pl.program_id(1)

    @pl.when(n == 0)
    def _():
        x = x_ref[...]
        y = x * lax.rsqrt(jnp.mean(x * x, axis=-1, keepdims=True) + EPS) * nw_ref[...]
        h = y * (1.0 + sc_ref[...]) + sh_ref[...]
        h_scr[...] = h.astype(jnp.bfloat16)
        dt_ref[...] = jnp.dot(h_scr[...], wdt_ref[...], preferred_element_type=jnp.float32)

    acc = jnp.dot(h_scr[...], w_ref[...], preferred_element_type=jnp.float32)

    def rope(t):
        reps = CHUNK_COLS // LANES
        c = jnp.tile(c_ref[...], (1, reps))
        sa = jnp.tile(sa_ref[...], (1, reps))
        sb = jnp.tile(sb_ref[...], (1, reps))
        half = ROPE_DIM // 2
        return (t * c + pltpu.roll(t, half, 1) * sa + pltpu.roll(t, CHUNK_COLS - half, 1) * sb)

    @pl.when(n == CH_Q)
    def _():
        p_ref[...] = (rope(acc) * (ATT_HEAD_DIM ** -0.5)).astype(p_ref.dtype)

    @pl.when(n == CH_K)
    def _():
        p_ref[...] = rope(acc).astype(p_ref.dtype)

    @pl.when(n > CH_K)
    def _():
        p_ref[...] = acc.astype(p_ref.dtype)


def _in_proj(x2, seq, norm_w, scale, shift, w_main, w_dt, rope_c, rope_sa, rope_sb):
    t, d = x2.shape
    tm = min(TM_PROJ, seq)
    per_b = seq // tm
    vec = lambda: pl.BlockSpec((None, 1, d), lambda m, n: (m // per_b, 0, 0))
    tab = lambda: pl.BlockSpec((tm, LANES), lambda m, n: (m, 0))
    return pl.pallas_call(
        _inproj_kernel,
        out_shape=(jax.ShapeDtypeStruct((N_CHUNKS, t, CHUNK_COLS), jnp.bfloat16),
                   jax.ShapeDtypeStruct((t, LANES), jnp.float32)),
        grid=(t // tm, N_CHUNKS),
        in_specs=[pl.BlockSpec((tm, d), lambda m, n: (m, 0)),
                  pl.BlockSpec((1, d), lambda m, n: (0, 0)),
                  vec(), vec(),
                  pl.BlockSpec((d, CHUNK_COLS), lambda m, n: (0, n)),
                  pl.BlockSpec((d, LANES), lambda m, n: (0, 0)),
                  tab(), tab(), tab()],
        out_specs=(pl.BlockSpec((None, tm, CHUNK_COLS), lambda m, n: (n, m, 0)),
                   pl.BlockSpec((tm, LANES), lambda m, n: (m, 0))),
        scratch_shapes=[pltpu.VMEM((tm, d), jnp.bfloat16)],
        compiler_params=_cparams(("arbitrary", "arbitrary")),
        name="in_proj",
    )(x2, norm_w.reshape(1, d), scale, shift, w_main, w_dt, rope_c, rope_sa, rope_sb)


def _attn_kernel(lam_ref, q_ref, k_ref, v_ref, g_ref, w_ref, o_ref, m_scr, l_scr, acc_scr,
                 *, tq, lambda_init):
    qi = pl.program_id(2)
    q = q_ref[...]
    lane = lax.broadcasted_iota(jnp.int32, q.shape, 1)
    zero = jnp.zeros_like(q)
    qs = jnp.concatenate([jnp.where(lane < ATT_HEAD_DIM, q, zero),
                          jnp.where(lane >= ATT_HEAD_DIM, q, zero)], axis=0)

    m_scr[...] = jnp.full(m_scr.shape, NEG_BIG, jnp.float32)
    l_scr[...] = jnp.zeros(l_scr.shape, jnp.float32)
    acc_scr[...] = jnp.zeros(acc_scr.shape, jnp.float32)

    def step(j, masked):
        start = pl.multiple_of(j * tq, tq)
        k = k_ref[pl.ds(start, tq), :]
        v = v_ref[pl.ds(start, tq), :]
        s = lax.dot_general(qs, k, (((1,), (1,)), ((), ())), preferred_element_type=jnp.float32)
        if masked:
            r = lax.broadcasted_iota(jnp.int32, (2 * tq, tq), 0)
            c = lax.broadcasted_iota(jnp.int32, (2 * tq, tq), 1)
            r = jnp.where(r >= tq, r - tq, r)
            s = jnp.where(c <= r, s, NEG_BIG)
        m_prev = m_scr[...]
        m_new = jnp.maximum(m_prev, jnp.max(s, axis=-1, keepdims=True))
        alpha = jnp.exp(m_prev - m_new)
        p = jnp.exp(s - m_new)
        l_scr[...] = alpha * l_scr[...] + jnp.sum(p, axis=-1, keepdims=True)
        acc_scr[...] = alpha * acc_scr[...] + jnp.dot(p.astype(jnp.bfloat16), v,
                                                       preferred_element_type=jnp.float32)
        m_scr[...] = m_new

    def body(j, carry):
        step(j, False)
        return carry

    lax.fori_loop(0, qi, body, 0)
    step(qi, True)

    prm = lam_ref[...]
    lam = (jnp.exp(jnp.sum(prm[0:1] * prm[1:2], axis=-1, keepdims=True))
           - jnp.exp(jnp.sum(prm[2:3] * prm[3:4], axis=-1, keepdims=True)) + lambda_init)
    o_all = acc_scr[...] / l_scr[...]
    o = o_all[:tq] - lam * o_all[tq:]
    o = o * lax.rsqrt(jnp.mean(o * o, axis=-1, keepdims=True) + EPS) * w_ref[...]
    o = o * (1.0 - lambda_init)
    o_ref[...] = (o * _silu(g_ref[...].astype(jnp.float32))).astype(o_ref.dtype)


def _diff_attention(p, bsz, seq, lam_prm, subln_w, lambda_init):
    t = bsz * seq
    tq = min(TQ, seq)
    nq = seq // tq
    blk = lambda ch: pl.BlockSpec((None, tq, LANES), lambda b, h, i: (ch, b * nq + i, h))
    full = lambda ch: pl.BlockSpec((None, seq, LANES), lambda b, h, i: (ch, b, h))
    return pl.pallas_call(
        functools.partial(_attn_kernel, tq=tq, lambda_init=lambda_init),
        out_shape=jax.ShapeDtypeStruct((t, ATT_WIDTH), jnp.bfloat16),
        grid=(bsz, ATT_HEADS, nq),
        in_specs=[pl.BlockSpec((SUBLANES, LANES), lambda b, h, i: (0, 0)),
                  blk(CH_Q), full(CH_K), full(CH_V), blk(CH_G),
                  pl.BlockSpec((1, LANES), lambda b, h, i: (0, 0))],
        out_specs=pl.BlockSpec((tq, LANES), lambda b, h, i: (b * nq + i, h)),
        scratch_shapes=[pltpu.VMEM((2 * tq, 1), jnp.float32),
                        pltpu.VMEM((2 * tq, 1), jnp.float32),
                        pltpu.VMEM((2 * tq, LANES), jnp.float32)],
        compiler_params=_cparams(("arbitrary", "arbitrary", "arbitrary")),
        name="diff_attn",
    )(lam_prm, p, p, p, p, subln_w.reshape(1, LANES))


def _ssd_kernel(xs0_ref, xs1_ref, bc_ref, z0_ref, z1_ref, dt_ref, cw_ref, cb_ref, dtb_ref,
                alog_ref, dsk_ref, nw_ref, e3_ref, o_ref, ext_scr, h_scr):
    L = SSM_CHUNK
    conv_ch = 3 * CHUNK_COLS
    halo = SUBLANES

    @pl.when(pl.program_id(1) == 0)
    def _():
        ext_scr[0:halo, :] = jnp.zeros((halo, conv_ch), jnp.float32)
        h_scr[...] = jnp.zeros(h_scr.shape, jnp.float32)

    ext_scr[halo:halo + L, 0:CHUNK_COLS] = xs0_ref[...].astype(jnp.float32)
    ext_scr[halo:halo + L, CHUNK_COLS:2 * CHUNK_COLS] = xs1_ref[...].astype(jnp.float32)
    ext_scr[halo:halo + L, 2 * CHUNK_COLS:] = bc_ref[...].astype(jnp.float32)
    conv = jnp.broadcast_to(cb_ref[...], (L, conv_ch))
    for tap in range(SSM_CONV):
        off = halo - (SSM_CONV - 1) + tap
        conv = conv + ext_scr[off:off + L, :] * cw_ref[tap:tap + 1, :]
    ext_scr[0:halo, :] = ext_scr[L:L + halo, :]
    act = _silu(conv)
    xs = act[:, :SSM_WIDTH]
    b_mat = act[:, SSM_WIDTH:SSM_WIDTH + SSM_GROUPS * SSM_STATE]
    c_mat = act[:, SSM_WIDTH + SSM_GROUPS * SSM_STATE:]

    dtr = dt_ref[...] + dtb_ref[...]
    dt = jnp.maximum(dtr, 0.0) + jnp.log1p(jnp.exp(-jnp.abs(dtr)))
    adt = dt * (-jnp.exp(alog_ref[...]))
    r_i = lax.broadcasted_iota(jnp.int32, (L, L), 0)
    c_i = lax.broadcasted_iota(jnp.int32, (L, L), 1)
    causal = c_i <= r_i
    tril = jnp.where(causal, 1.0, 0.0).astype(jnp.bfloat16)
    hi, mid, lo = _split3(adt)
    acs = jnp.dot(jnp.concatenate([tril, tril, tril], axis=1),
                  jnp.concatenate([hi, mid, lo], axis=0),
                  preferred_element_type=jnp.float32)
    acs_t = acs.T

    both = jnp.concatenate([dt, acs], axis=0)
    hi, mid, lo = _split3(both)
    ex = jnp.dot(jnp.concatenate([hi, mid, lo], axis=1), e3_ref[...],
                 preferred_element_type=jnp.float32)
    dt_e = ex[:L]
    acs_e = ex[L:]
    acs_last = acs_e[L - 1:L, :]
    eacs_e = jnp.exp(acs_e)
    xd = xs * dt_e
    xdd = (xd * jnp.exp(acs_last - acs_e)).astype(jnp.bfloat16)
    chunk_decay = jnp.exp(acs_last)

    lane = lax.broadcasted_iota(jnp.int32, (L, LANES), 1)
    b_t = b_mat.T
    y_groups = []
    for g in range(SSM_GROUPS):
        gs = slice(g * GROUP_WIDTH, (g + 1) * GROUP_WIDTH)
        c_g = c_mat[:, g * SSM_STATE:(g + 1) * SSM_STATE].astype(jnp.bfloat16)
        bt_g = b_t[g * SSM_STATE:(g + 1) * SSM_STATE, :].astype(jnp.bfloat16)
        cb = jnp.dot(c_g, bt_g, preferred_element_type=jnp.float32)

        h_g = h_scr[:, gs]
        y_off = jnp.dot(c_g, h_g.astype(jnp.bfloat16),
                        preferred_element_type=jnp.float32) * eacs_e[:, gs]
        st = jnp.dot(bt_g, xdd[:, gs], preferred_element_type=jnp.float32)
        h_scr[:, gs] = h_g * chunk_decay[:, gs] + st

        parts = []
        for pr in range(HEADS_PER_GROUP // 2):
            col0 = g * GROUP_WIDTH + pr * LANES
            xd_pair = xd[:, col0:col0 + LANES]
            ms = []
            for hh in range(2):
                j = g * HEADS_PER_GROUP + 2 * pr + hh
                seg = acs[:, j:j + 1] - acs_t[j:j + 1, :]
                lm = jnp.exp(jnp.where(causal, seg, NEG_BIG))
                ms.append((cb * lm).astype(jnp.bfloat16))
            w_pair = jnp.concatenate(
                [jnp.where(lane < SSM_HEAD_DIM, xd_pair, 0.0),
                 jnp.where(lane >= SSM_HEAD_DIM, xd_pair, 0.0)], axis=0).astype(jnp.bfloat16)
            parts.append(jnp.dot(jnp.concatenate(ms, axis=1), w_pair,
                                 preferred_element_type=jnp.float32))
        y_diag = jnp.concatenate(parts, axis=1)

        y = y_diag + y_off + dsk_ref[:, gs] * xs[:, gs]
        zg = (z0_ref if g == 0 else z1_ref)[...].astype(jnp.float32)
        y = y * _silu(zg)
        y = y * lax.rsqrt(jnp.mean(y * y, axis=-1, keepdims=True) + EPS) * nw_ref[:, gs]
        y_groups.append(y)
    o_ref[...] = jnp.concatenate(y_groups, axis=1).astype(o_ref.dtype)


def _ssd_branch(p, dt_raw, bsz, seq, conv_w, conv_b, dt_bias, a_log, d_skip, ssm_norm_w, e3):
    t = bsz * seq
    L = SSM_CHUNK
    nc = seq // L
    conv_ch = 3 * CHUNK_COLS
    blk = lambda ch: pl.BlockSpec((None, L, CHUNK_COLS), lambda b, c: (ch, b * nc + c, 0))
    row = lambda w: pl.BlockSpec((1, w), lambda b, c: (0, 0))
    pad16 = lambda v: jnp.pad(v.astype(jnp.float32), (0, LANES - SSM_HEADS)).reshape(1, LANES)
    return pl.pallas_call(
        _ssd_kernel,
        out_shape=jax.ShapeDtypeStruct((t, SSM_WIDTH), jnp.bfloat16),
        grid=(bsz, nc),
        in_specs=[blk(CH_XS), blk(CH_XS + 1), blk(CH_BC), blk(CH_Z), blk(CH_Z + 1),
                  pl.BlockSpec((L, LANES), lambda b, c: (b * nc + c, 0)),
                  pl.BlockSpec((SSM_CONV, conv_ch), lambda b, c: (0, 0)),
                  row(conv_ch), row(LANES), row(LANES), row(SSM_WIDTH), row(SSM_WIDTH),
                  pl.BlockSpec((3 * LANES, SSM_WIDTH), lambda b, c: (0, 0))],
        out_specs=pl.BlockSpec((L, SSM_WIDTH), lambda b, c: (b * nc + c, 0)),
        scratch_shapes=[pltpu.VMEM((L + SUBLANES, conv_ch), jnp.float32),
                        pltpu.VMEM((SSM_STATE, SSM_WIDTH), jnp.float32)],
        compiler_params=_cparams(("arbitrary", "arbitrary")),
        name="ssd_branch",
    )(p, p, p, p, p, dt_raw, conv_w, conv_b.reshape(1, conv_ch), pad16(dt_bias), pad16(a_log),
      jnp.repeat(d_skip.astype(jnp.float32), SSM_HEAD_DIM).reshape(1, SSM_WIDTH),
      ssm_norm_w.reshape(1, SSM_WIDTH), e3)


def _merge_kernel(*refs, final):
    if final:
        (oa_ref, ys_ref, ga0_ref, ga1_ref, gs0_ref, gs1_ref, x_ref, gate_ref,
         wa_ref, ws_ref, wo_ref, fw_ref, o_ref) = refs
    else:
        (oa_ref, ys_ref, ga0_ref, ga1_ref, gs0_ref, gs1_ref, x_ref, gate_ref,
         wa_ref, ws_ref, wo_ref, o_ref) = refs
    y_att = jnp.dot(oa_ref[...], wa_ref[...], preferred_element_type=jnp.float32)
    y_ssm = jnp.dot(ys_ref[...], ws_ref[...], preferred_element_type=jnp.float32)
    mg_att = jnp.concatenate([ga0_ref[...], ga1_ref[...]], axis=1).astype(jnp.float32)
    mg_ssm = jnp.concatenate([gs0_ref[...], gs1_ref[...]], axis=1).astype(jnp.float32)
    merged = _sigmoid(mg_att) * y_att + _sigmoid(mg_ssm) * y_ssm
    out = x_ref[...] + gate_ref[...] * jnp.dot(merged.astype(jnp.bfloat16), wo_ref[...],
                                                preferred_element_type=jnp.float32)
    if final:
        out = out * lax.rsqrt(jnp.mean(out * out, axis=-1, keepdims=True) + EPS) * fw_ref[...]
    o_ref[...] = out


def _merge_out(o_att, y_ssm, p, x2, seq, gate, w_att, w_ssm, w_out, final_w):
    t, d = x2.shape
    tm = min(TM_OUT, seq)
    per_b = seq // tm
    final = final_w is not None
    chunk = lambda ch: pl.BlockSpec((None, tm, CHUNK_COLS), lambda m: (ch, m, 0))
    res = lambda shape: pl.BlockSpec(shape, lambda m: (0, 0))
    in_specs = [pl.BlockSpec((tm, ATT_WIDTH), lambda m: (m, 0)),
                pl.BlockSpec((tm, SSM_WIDTH), lambda m: (m, 0)),
                chunk(CH_MGA), chunk(CH_MGA + 1), chunk(CH_MGS), chunk(CH_MGS + 1),
                pl.BlockSpec((tm, d), lambda m: (m, 0)),
                pl.BlockSpec((None, 1, d), lambda m: (m // per_b, 0, 0)),
                res((ATT_WIDTH, d)), res((SSM_WIDTH, d)), res((d, d))]
    args = [o_att, y_ssm, p, p, p, p, x2, gate, w_att, w_ssm, w_out]
    if final:
        in_specs.append(res((1, d)))
        args.append(final_w.reshape(1, d))
    return pl.pallas_call(
        functools.partial(_merge_kernel, final=final),
        out_shape=jax.ShapeDtypeStruct((t, d), jnp.float32),
        grid=(t // tm,),
        in_specs=in_specs,
        out_specs=pl.BlockSpec((tm, d), lambda m: (m, 0)),
        compiler_params=_cparams(("arbitrary",)),
        name="merge_out",
    )(*args)


def _reorder_w_in(w_in_l):
    sizes = [ATT_WIDTH, ATT_WIDTH, ATT_WIDTH, ATT_WIDTH, SSM_WIDTH,
             SSM_WIDTH + 2 * SSM_GROUPS * SSM_STATE, SSM_HEADS, D_MODEL, D_MODEL]
    offs = np.cumsum([0] + sizes)
    seg = lambda i: w_in_l[:, offs[i]:offs[i + 1]]
    q, k, v, g, z, xbc, dtc, mga, mgs = [seg(i) for i in range(9)]
    w_main = jnp.concatenate([q, k, v, g, z, xbc, mga, mgs], axis=1).astype(jnp.bfloat16)
    w_dt = jnp.pad(dtc, ((0, 0), (0, LANES - SSM_HEADS))).astype(jnp.bfloat16)
    return w_main, w_dt


def _expand_matrix():
    e = np.zeros((LANES, SSM_WIDTH), np.float32)
    for j in range(SSM_HEADS):
        e[j, j * SSM_HEAD_DIM:(j + 1) * SSM_HEAD_DIM] = 1.0
    return jnp.asarray(np.concatenate([e, e, e], axis=0), dtype=jnp.bfloat16)


def kernel(x, c, positions, w_ada, b_ada, norm_w, w_in, lambda_q1, lambda_k1, lambda_q2, lambda_k2,
           attn_subln_w, w_att_branch, conv_w, conv_b, dt_bias, a_log, d_skip, ssm_norm_w,
           w_ssm_branch, w_out, final_norm_w):
    bsz, seq, d = x.shape
    depth = w_ada.shape[0]
    t = bsz * seq
    x2 = x.reshape(t, d)

    c_pad = jnp.pad(c, ((0, SUBLANES - bsz % SUBLANES), (0, 0))) if bsz % SUBLANES else c
    mod = _ada_mod(c_pad, w_ada, b_ada)[:, :bsz]
    rope_c, rope_sa, rope_sb = _rope_tables(positions)
    e3 = _expand_matrix()

    for l in range(depth):
        lambda_init = 0.8 - 0.6 * math.exp(-0.3 * l)
        shift = mod[l, :, :d].reshape(bsz, 1, d)
        scale = mod[l, :, d:2 * d].reshape(bsz, 1, d)
        gate = mod[l, :, 2 * d:].reshape(bsz, 1, d)
        w_main, w_dt = _reorder_w_in(w_in[l])
        p, dt_raw = _in_proj(x2, seq, norm_w[l], scale, shift, w_main, w_dt,
                             rope_c, rope_sa, rope_sb)
        lam_prm = jnp.zeros((SUBLANES, LANES), jnp.float32)
        lam_prm = lam_prm.at[0:4, :ATT_HEAD_DIM].set(
            jnp.stack([lambda_q1[l], lambda_k1[l], lambda_q2[l], lambda_k2[l]]))
        o_att = _diff_attention(p, bsz, seq, lam_prm, attn_subln_w[l], lambda_init)
        y_ssm = _ssd_branch(p, dt_raw, bsz, seq, conv_w[l], conv_b[l], dt_bias[l], a_log[l],
                            d_skip[l], ssm_norm_w[l], e3)
        x2 = _merge_out(o_att, y_ssm, p, x2, seq, gate,
                        w_att_branch[l].astype(jnp.bfloat16), w_ssm_branch[l].astype(jnp.bfloat16),
                        w_out[l].astype(jnp.bfloat16),
                        final_norm_w if l == depth - 1 else None)
    return x2.reshape(bsz, seq, d)
```

```python
import functools
import math

import numpy as np
import jax
import jax.numpy as jnp
from jax import lax
from jax.experimental import pallas as pl
from jax.experimental.pallas import tpu as pltpu

D_MODEL = 1024
DEPTH = 2
ATT_HEADS = 4
ATT_HEAD_DIM = 64
ATT_V_DIM = 2 * ATT_HEAD_DIM
ATT_WIDTH = ATT_HEADS * ATT_V_DIM
ROPE_THETA = 500000.0
ROPE_DIM = ATT_HEAD_DIM // 4
SSM_HEAD_DIM = 64
SSM_WIDTH = D_MODEL
SSM_HEADS = SSM_WIDTH // SSM_HEAD_DIM
SSM_GROUPS = 2
SSM_STATE = 128
SSM_CONV = 4
SSM_CHUNK = 128
GROUP_WIDTH = SSM_WIDTH // SSM_GROUPS
HEADS_PER_GROUP = SSM_HEADS // SSM_GROUPS
EPS = 1e-5

LANES = 128
SUBLANES = 8
CHUNK_COLS = 512
CH_Q, CH_K, CH_V, CH_G = 0, 1, 2, 3
CH_Z = 4
CH_XS = 6
CH_BC = 8
CH_MGA = 9
CH_MGS = 11
N_CHUNKS = 13
LOG2_E = math.log2(math.e)
NEG_BIG = -1e30

VMEM_LIMIT = 48 * 1024 * 1024

TM_PROJ = 512
TQ = 512
ATT_ROWS = 256
TM_OUT = 512


def _cparams(sem):
    return pltpu.CompilerParams(dimension_semantics=sem, vmem_limit_bytes=VMEM_LIMIT)


def _sigmoid(x):
    return 1.0 / (1.0 + jnp.exp(-x))


def _silu(x):
    return x * _sigmoid(x)


def _split3(x):
    hi = x.astype(jnp.bfloat16)
    r1 = x - hi.astype(jnp.float32)
    mid = r1.astype(jnp.bfloat16)
    lo = (r1 - mid.astype(jnp.float32)).astype(jnp.bfloat16)
    return hi, mid, lo


def _ada_kernel(c_ref, w_ref, b_ref, o_ref):
    o_ref[...] = jnp.dot(c_ref[...], w_ref[...], preferred_element_type=jnp.float32,
                         precision=lax.Precision.HIGHEST) + b_ref[...]


def _ada_mod(c_pad, w_ada, b_ada):
    depth, d, d3 = w_ada.shape
    rows = c_pad.shape[0]
    nblk = d3 // d
    return pl.pallas_call(
        _ada_kernel,
        out_shape=jax.ShapeDtypeStruct((depth, rows, d3), jnp.float32),
        grid=(depth, nblk),
        in_specs=[pl.BlockSpec((rows, d), lambda l, n: (0, 0)),
                  pl.BlockSpec((None, d, d), lambda l, n: (l, 0, n)),
                  pl.BlockSpec((None, 1, d), lambda l, n: (l, 0, n))],
        out_specs=pl.BlockSpec((None, rows, d), lambda l, n: (l, 0, n)),
        compiler_params=_cparams(("arbitrary", "arbitrary")),
        name="ada_mod",
    )(c_pad, w_ada, b_ada.reshape(depth, 1, d3))


def _rope_kernel(pos_ref, freq_ref, ma_ref, mb_ref, c_ref, sa_ref, sb_ref):
    ang = pos_ref[...].astype(jnp.float32) * freq_ref[...]
    sn = jnp.sin(ang)
    c_ref[...] = jnp.cos(ang)
    sa_ref[...] = sn * ma_ref[...]
    sb_ref[...] = sn * mb_ref[...]


def _rope_tables(positions):
    t = positions.size
    half = ROPE_DIM // 2
    inv_freq = ROPE_THETA ** (-jnp.arange(0, ROPE_DIM, 2, dtype=jnp.float32) / ROPE_DIM)
    d = np.arange(LANES) % ATT_HEAD_DIM
    in_rope = d < ROPE_DIM
    freq = jnp.where(in_rope, inv_freq[d % half], 0.0).astype(jnp.float32).reshape(1, LANES)
    ma = jnp.asarray((in_rope & (d >= half)).astype(np.float32)).reshape(1, LANES)
    mb = jnp.asarray(-(d < half).astype(np.float32)).reshape(1, LANES)
    pos_b = jnp.broadcast_to(positions.reshape(t, 1), (t, LANES))
    tr = 1024
    row = pl.BlockSpec((1, LANES), lambda i: (0, 0))
    tab = pl.BlockSpec((tr, LANES), lambda i: (i, 0))
    sds = jax.ShapeDtypeStruct((t, LANES), jnp.float32)
    return pl.pallas_call(
        _rope_kernel, out_shape=(sds, sds, sds), grid=(t // tr,),
        in_specs=[tab, row, row, row], out_specs=(tab, tab, tab),
        compiler_params=_cparams(("arbitrary",)), name="rope_tables",
    )(pos_b, freq, ma, mb)


def _inproj_kernel(x_ref, nw_ref, sc_ref, sh_ref, w_ref, wdt_ref, c_ref, sa_ref, sb_ref,
                   p_ref, dt_ref, h_scr):
    n = pl.program_id(1)

    @pl.when(n == 0)
    def _():
        x = x_ref[...]
        y = x * lax.rsqrt(jnp.mean(x * x, axis=-1, keepdims=True) + EPS) * nw_ref[...]
        h = y * (1.0 + sc_ref[...]) + sh_ref[...]
        h_scr[...] = h.astype(jnp.bfloat16)
        dt_ref[...] = jnp.dot(h_scr[...], wdt_ref[...], preferred_element_type=jnp.float32)

    acc = jnp.dot(h_scr[...], w_ref[...], preferred_element_type=jnp.float32)

    def rope(t):
        reps = CHUNK_COLS // LANES
        c = jnp.tile(c_ref[...], (1, reps))
        sa = jnp.tile(sa_ref[...], (1, reps))
        sb = jnp.tile(sb_ref[...], (1, reps))
        half = ROPE_DIM // 2
        return (t * c + pltpu.roll(t, half, 1) * sa + pltpu.roll(t, CHUNK_COLS - half, 1) * sb)

    @pl.when(n == CH_Q)
    def _():
        p_ref[...] = (rope(acc) * (ATT_HEAD_DIM ** -0.5 * LOG2_E)).astype(p_ref.dtype)

    @pl.when(n == CH_K)
    def _():
        p_ref[...] = rope(acc).astype(p_ref.dtype)

    @pl.when(n > CH_K)
    def _():
        p_ref[...] = acc.astype(p_ref.dtype)


def _in_proj(x2, seq, norm_w, scale, shift, w_main, w_dt, rope_c, rope_sa, rope_sb):
    t, d = x2.shape
    tm = min(TM_PROJ, seq)
    per_b = seq // tm
    vec = lambda: pl.BlockSpec((None, 1, d), lambda m, n: (m // per_b, 0, 0))
    tab = lambda: pl.BlockSpec((tm, LANES), lambda m, n: (m, 0))
    return pl.pallas_call(
        _inproj_kernel,
        out_shape=(jax.ShapeDtypeStruct((N_CHUNKS, t, CHUNK_COLS), jnp.bfloat16),
                   jax.ShapeDtypeStruct((t, LANES), jnp.float32)),
        grid=(t // tm, N_CHUNKS),
        in_specs=[pl.BlockSpec((tm, d), lambda m, n: (m, 0)),
                  pl.BlockSpec((1, d), lambda m, n: (0, 0)),
                  vec(), vec(),
                  pl.BlockSpec((d, CHUNK_COLS), lambda m, n: (0, n)),
                  pl.BlockSpec((d, LANES), lambda m, n: (0, 0)),
                  tab(), tab(), tab()],
        out_specs=(pl.BlockSpec((None, tm, CHUNK_COLS), lambda m, n: (n, m, 0)),
                   pl.BlockSpec((tm, LANES), lambda m, n: (m, 0))),
        scratch_shapes=[pltpu.VMEM((tm, d), jnp.bfloat16)],
        compiler_params=_cparams(("arbitrary", "arbitrary")),
        name="in_proj",
    )(x2, norm_w.reshape(1, d), scale, shift, w_main, w_dt, rope_c, rope_sa, rope_sb)


def _attn_kernel(lam_ref, q_ref, k_ref, v_ref, g_ref, w_ref, o_ref, qs_scr, vx_scr, m_scr, acc_scr,
                 *, tq, lambda_init):
    qi = pl.program_id(2)

    @pl.when(qi == 0)
    def _():
        vx_scr[:, :LANES] = v_ref[...]
        vx_scr[:, LANES:] = jnp.ones((vx_scr.shape[0], LANES), vx_scr.dtype)

    q = q_ref[...]
    lane = lax.broadcasted_iota(jnp.int32, q.shape, 1)
    zero = jnp.zeros_like(q)
    qs_scr[0:tq, :] = jnp.where(lane < ATT_HEAD_DIM, q, zero)
    qs_scr[tq:, :] = jnp.where(lane >= ATT_HEAD_DIM, q, zero)
    m_scr[...] = jnp.full(m_scr.shape, NEG_BIG, jnp.float32)
    acc_scr[...] = jnp.zeros(acc_scr.shape, jnp.float32)

    def chain(rc, start, diagonal):
        rows = slice(rc * ATT_ROWS, (rc + 1) * ATT_ROWS)
        row0 = (rc * ATT_ROWS) % tq
        ncol = row0 + ATT_ROWS if diagonal else tq
        k = k_ref[pl.ds(start, ncol), :]
        s = lax.dot_general(qs_scr[rows, :], k, (((1,), (1,)), ((), ())),
                            preferred_element_type=jnp.float32)
        if diagonal:
            r = lax.broadcasted_iota(jnp.int32, s.shape, 0) + row0
            c = lax.broadcasted_iota(jnp.int32, s.shape, 1)
            s = jnp.where(c <= r, s, NEG_BIG)
        m_prev = m_scr[rows, :]
        m_new = jnp.maximum(m_prev, jnp.max(s, axis=-1, keepdims=True))
        alpha = jnp.exp2(m_prev - m_new)
        p = jnp.exp2(s - jnp.tile(m_new, (1, ncol // LANES)))
        pv = jnp.dot(p.astype(jnp.bfloat16), vx_scr[pl.ds(start, ncol), :],
                     preferred_element_type=jnp.float32)
        acc_scr[rows, :] = jnp.tile(alpha, (1, 2)) * acc_scr[rows, :] + pv
        m_scr[rows, :] = m_new

    n_chains = 2 * tq // ATT_ROWS

    def body(j, carry):
        start = pl.multiple_of(j * tq, tq)
        for rc in range(n_chains):
            chain(rc, start, False)
        return carry

    lax.fori_loop(0, qi, body, 0)
    start = pl.multiple_of(qi * tq, tq)
    for rc in range(n_chains):
        chain(rc, start, True)

    prm = lam_ref[...]
    lam = (jnp.exp(jnp.sum(prm[0:1] * prm[1:2], axis=-1, keepdims=True))
           - jnp.exp(jnp.sum(prm[2:3] * prm[3:4], axis=-1, keepdims=True)) + lambda_init)
    o_all = acc_scr[:, :LANES] / acc_scr[:, LANES:]
    o = o_all[:tq] - lam * o_all[tq:]
    o = o * lax.rsqrt(jnp.mean(o * o, axis=-1, keepdims=True) + EPS) * w_ref[...]
    o = o * (1.0 - lambda_init)
    o_ref[...] = (o * _silu(g_ref[...].astype(jnp.float32))).astype(o_ref.dtype)


def _diff_attention(p, bsz, seq, lam_prm, subln_w, lambda_init):
    t = bsz * seq
    tq = min(TQ, seq)
    nq = seq // tq
    blk = lambda ch: pl.BlockSpec((None, tq, LANES), lambda b, h, i: (ch, b * nq + i, h))
    full = lambda ch: pl.BlockSpec((None, seq, LANES), lambda b, h, i: (ch, b, h))
    return pl.pallas_call(
        functools.partial(_attn_kernel, tq=tq, lambda_init=lambda_init),
        out_shape=jax.ShapeDtypeStruct((t, ATT_WIDTH), jnp.bfloat16),
        grid=(bsz, ATT_HEADS, nq),
        in_specs=[pl.BlockSpec((SUBLANES, LANES), lambda b, h, i: (0, 0)),
                  blk(CH_Q), full(CH_K), full(CH_V), blk(CH_G),
                  pl.BlockSpec((1, LANES), lambda b, h, i: (0, 0))],
        out_specs=pl.BlockSpec((tq, LANES), lambda b, h, i: (b * nq + i, h)),
        scratch_shapes=[pltpu.VMEM((2 * tq, LANES), jnp.bfloat16),
                        pltpu.VMEM((seq, 2 * LANES), jnp.bfloat16),
                        pltpu.VMEM((2 * tq, LANES), jnp.float32),
                        pltpu.VMEM((2 * tq, 2 * LANES), jnp.float32)],
        compiler_params=_cparams(("arbitrary", "arbitrary", "arbitrary")),
        name="diff_attn",
    )(lam_prm, p, p, p, p, subln_w.reshape(1, LANES))


def _ssd_kernel(xs0_ref, xs1_ref, bc_ref, z0_ref, z1_ref, dt_ref, cw_ref, cb_ref, dtb_ref,
                alog_ref, dsk_ref, nw_ref, e3_ref, o_ref, ext_scr, h_scr):
    L = SSM_CHUNK
    conv_ch = 3 * CHUNK_COLS
    halo = SUBLANES

    @pl.when(pl.program_id(1) == 0)
    def _():
        ext_scr[0:halo, :] = jnp.zeros((halo, conv_ch), jnp.float32)
        h_scr[...] = jnp.zeros(h_scr.shape, jnp.float32)

    ext_scr[halo:halo + L, 0:CHUNK_COLS] = xs0_ref[...].astype(jnp.float32)
    ext_scr[halo:halo + L, CHUNK_COLS:2 * CHUNK_COLS] = xs1_ref[...].astype(jnp.float32)
    ext_scr[halo:halo + L, 2 * CHUNK_COLS:] = bc_ref[...].astype(jnp.float32)
    conv = jnp.broadcast_to(cb_ref[...], (L, conv_ch))
    for tap in range(SSM_CONV):
        off = halo - (SSM_CONV - 1) + tap
        conv = conv + ext_scr[off:off + L, :] * cw_ref[tap:tap + 1, :]
    ext_scr[0:halo, :] = ext_scr[L:L + halo, :]
    act = _silu(conv)
    xs = act[:, :SSM_WIDTH]
    b_mat = act[:, SSM_WIDTH:SSM_WIDTH + SSM_GROUPS * SSM_STATE]
    c_mat = act[:, SSM_WIDTH + SSM_GROUPS * SSM_STATE:]

    dtr = dt_ref[...] + dtb_ref[...]
    dt = jnp.maximum(dtr, 0.0) + jnp.log1p(jnp.exp(-jnp.abs(dtr)))
    adt = dt * (-jnp.exp(alog_ref[...]))
    r_i = lax.broadcasted_iota(jnp.int32, (L, L), 0)
    c_i = lax.broadcasted_iota(jnp.int32, (L, L), 1)
    causal = c_i <= r_i
    tril = jnp.where(causal, 1.0, 0.0).astype(jnp.bfloat16)
    hi, mid, lo = _split3(adt)
    acs = jnp.dot(jnp.concatenate([tril, tril, tril], axis=1),
                  jnp.concatenate([hi, mid, lo], axis=0),
                  preferred_element_type=jnp.float32)
    acs_t = acs.T

    both = jnp.concatenate([dt, acs], axis=0)
    hi, mid, lo = _split3(both)
    ex = jnp.dot(jnp.concatenate([hi, mid, lo], axis=1), e3_ref[...],
                 preferred_element_type=jnp.float32)
    dt_e = ex[:L]
    acs_e = ex[L:]
    acs_last = acs_e[L - 1:L, :]
    eacs_e = jnp.exp(acs_e)
    xd = xs * dt_e
    xdd = (xd * jnp.exp(acs_last - acs_e)).astype(jnp.bfloat16)
    chunk_decay = jnp.exp(acs_last)

    lane = lax.broadcasted_iota(jnp.int32, (L, LANES), 1)
    b_t = b_mat.T
    y_groups = []
    for g in range(SSM_GROUPS):
        gs = slice(g * GROUP_WIDTH, (g + 1) * GROUP_WIDTH)
        c_g = c_mat[:, g * SSM_STATE:(g + 1) * SSM_STATE].astype(jnp.bfloat16)
        bt_g = b_t[g * SSM_STATE:(g + 1) * SSM_STATE, :].astype(jnp.bfloat16)
        cb = jnp.dot(c_g, bt_g, preferred_element_type=jnp.float32)

        h_g = h_scr[:, gs]
        y_off = jnp.dot(c_g, h_g.astype(jnp.bfloat16),
                        preferred_element_type=jnp.float32) * eacs_e[:, gs]
        st = jnp.dot(bt_g, xdd[:, gs], preferred_element_type=jnp.float32)
        h_scr[:, gs] = h_g * chunk_decay[:, gs] + st

        parts = []
        for pr in range(HEADS_PER_GROUP // 2):
            col0 = g * GROUP_WIDTH + pr * LANES
            xd_pair = xd[:, col0:col0 + LANES]
            ms = []
            for hh in range(2):
                j = g * HEADS_PER_GROUP + 2 * pr + hh
                seg = acs[:, j:j + 1] - acs_t[j:j + 1, :]
                lm = jnp.exp(jnp.where(causal, seg, NEG_BIG))
                ms.append((cb * lm).astype(jnp.bfloat16))
            w_pair = jnp.concatenate(
                [jnp.where(lane < SSM_HEAD_DIM, xd_pair, 0.0),
                 jnp.where(lane >= SSM_HEAD_DIM, xd_pair, 0.0)], axis=0).astype(jnp.bfloat16)
            parts.append(jnp.dot(jnp.concatenate(ms, axis=1), w_pair,
                                 preferred_element_type=jnp.float32))
        y_diag = jnp.concatenate(parts, axis=1)

        y = y_diag + y_off + dsk_ref[:, gs] * xs[:, gs]
        zg = (z0_ref if g == 0 else z1_ref)[...].astype(jnp.float32)
        y = y * _silu(zg)
        y = y * lax.rsqrt(jnp.mean(y * y, axis=-1, keepdims=True) + EPS) * nw_ref[:, gs]
        y_groups.append(y)
    o_ref[...] = jnp.concatenate(y_groups, axis=1).astype(o_ref.dtype)


def _ssd_branch(p, dt_raw, bsz, seq, conv_w, conv_b, dt_bias, a_log, d_skip, ssm_norm_w, e3):
    t = bsz * seq
    L = SSM_CHUNK
    nc = seq // L
    conv_ch = 3 * CHUNK_COLS
    blk = lambda ch: pl.BlockSpec((None, L, CHUNK_COLS), lambda b, c: (ch, b * nc + c, 0))
    row = lambda w: pl.BlockSpec((1, w), lambda b, c: (0, 0))
    pad16 = lambda v: jnp.pad(v.astype(jnp.float32), (0, LANES - SSM_HEADS)).reshape(1, LANES)
    return pl.pallas_call(
        _ssd_kernel,
        out_shape=jax.ShapeDtypeStruct((t, SSM_WIDTH), jnp.bfloat16),
        grid=(bsz, nc),
        in_specs=[blk(CH_XS), blk(CH_XS + 1), blk(CH_BC), blk(CH_Z), blk(CH_Z + 1),
                  pl.BlockSpec((L, LANES), lambda b, c: (b * nc + c, 0)),
                  pl.BlockSpec((SSM_CONV, conv_ch), lambda b, c: (0, 0)),
                  row(conv_ch), row(LANES), row(LANES), row(SSM_WIDTH), row(SSM_WIDTH),
                  pl.BlockSpec((3 * LANES, SSM_WIDTH), lambda b, c: (0, 0))],
        out_specs=pl.BlockSpec((L, SSM_WIDTH), lambda b, c: (b * nc + c, 0)),
        scratch_shapes=[pltpu.VMEM((L + SUBLANES, conv_ch), jnp.float32),
                        pltpu.VMEM((SSM_STATE, SSM_WIDTH), jnp.float32)],
        compiler_params=_cparams(("arbitrary", "arbitrary")),
        name="ssd_branch",
    )(p, p, p, p, p, dt_raw, conv_w, conv_b.reshape(1, conv_ch), pad16(dt_bias), pad16(a_log),
      jnp.repeat(d_skip.astype(jnp.float32), SSM_HEAD_DIM).reshape(1, SSM_WIDTH),
      ssm_norm_w.reshape(1, SSM_WIDTH), e3)


def _merge_kernel(*refs, final):
    if final:
        (oa_ref, ys_ref, ga0_ref, ga1_ref, gs0_ref, gs1_ref, x_ref, gate_ref,
         wa_ref, ws_ref, wo_ref, fw_ref, o_ref) = refs
    else:
        (oa_ref, ys_ref, ga0_ref, ga1_ref, gs0_ref, gs1_ref, x_ref, gate_ref,
         wa_ref, ws_ref, wo_ref, o_ref) = refs
    y_att = jnp.dot(oa_ref[...], wa_ref[...], preferred_element_type=jnp.float32)
    y_ssm = jnp.dot(ys_ref[...], ws_ref[...], preferred_element_type=jnp.float32)
    mg_att = jnp.concatenate([ga0_ref[...], ga1_ref[...]], axis=1).astype(jnp.float32)
    mg_ssm = jnp.concatenate([gs0_ref[...], gs1_ref[...]], axis=1).astype(jnp.float32)
    merged = _sigmoid(mg_att) * y_att + _sigmoid(mg_ssm) * y_ssm
    out = x_ref[...] + gate_ref[...] * jnp.dot(merged.astype(jnp.bfloat16), wo_ref[...],
                                                preferred_element_type=jnp.float32)
    if final:
        out = out * lax.rsqrt(jnp.mean(out * out, axis=-1, keepdims=True) + EPS) * fw_ref[...]
    o_ref[...] = out


def _merge_out(o_att, y_ssm, p, x2, seq, gate, w_att, w_ssm, w_out, final_w):
    t, d = x2.shape
    tm = min(TM_OUT, seq)
    per_b = seq // tm
    final = final_w is not None
    chunk = lambda ch: pl.BlockSpec((None, tm, CHUNK_COLS), lambda m: (ch, m, 0))
    res = lambda shape: pl.BlockSpec(shape, lambda m: (0, 0))
    in_specs = [pl.BlockSpec((tm, ATT_WIDTH), lambda m: (m, 0)),
                pl.BlockSpec((tm, SSM_WIDTH), lambda m: (m, 0)),
                chunk(CH_MGA), chunk(CH_MGA + 1), chunk(CH_MGS), chunk(CH_MGS + 1),
                pl.BlockSpec((tm, d), lambda m: (m, 0)),
                pl.BlockSpec((None, 1, d), lambda m: (m // per_b, 0, 0)),
                res((ATT_WIDTH, d)), res((SSM_WIDTH, d)), res((d, d))]
    args = [o_att, y_ssm, p, p, p, p, x2, gate, w_att, w_ssm, w_out]
    if final:
        in_specs.append(res((1, d)))
        args.append(final_w.reshape(1, d))
    return pl.pallas_call(
        functools.partial(_merge_kernel, final=final),
        out_shape=jax.ShapeDtypeStruct((t, d), jnp.float32),
        grid=(t // tm,),
        in_specs=in_specs,
        out_specs=pl.BlockSpec((tm, d), lambda m: (m, 0)),
        compiler_params=_cparams(("arbitrary",)),
        name="merge_out",
    )(*args)


def _reorder_w_in(w_in_l):
    sizes = [ATT_WIDTH, ATT_WIDTH, ATT_WIDTH, ATT_WIDTH, SSM_WIDTH,
             SSM_WIDTH + 2 * SSM_GROUPS * SSM_STATE, SSM_HEADS, D_MODEL, D_MODEL]
    offs = np.cumsum([0] + sizes)
    seg = lambda i: w_in_l[:, offs[i]:offs[i + 1]]
    q, k, v, g, z, xbc, dtc, mga, mgs = [seg(i) for i in range(9)]
    w_main = jnp.concatenate([q, k, v, g, z, xbc, mga, mgs], axis=1).astype(jnp.bfloat16)
    w_dt = jnp.pad(dtc, ((0, 0), (0, LANES - SSM_HEADS))).astype(jnp.bfloat16)
    return w_main, w_dt


def _expand_matrix():
    e = np.zeros((LANES, SSM_WIDTH), np.float32)
    for j in range(SSM_HEADS):
        e[j, j * SSM_HEAD_DIM:(j + 1) * SSM_HEAD_DIM] = 1.0
    return jnp.asarray(np.concatenate([e, e, e], axis=0), dtype=jnp.bfloat16)


def kernel(x, c, positions, w_ada, b_ada, norm_w, w_in, lambda_q1, lambda_k1, lambda_q2, lambda_k2,
           attn_subln_w, w_att_branch, conv_w, conv_b, dt_bias, a_log, d_skip, ssm_norm_w,
           w_ssm_branch, w_out, final_norm_w):
    bsz, seq, d = x.shape
    depth = w_ada.shape[0]
    t = bsz * seq
    x2 = x.reshape(t, d)

    c_pad = jnp.pad(c, ((0, SUBLANES - bsz % SUBLANES), (0, 0))) if bsz % SUBLANES else c
    mod = _ada_mod(c_pad, w_ada, b_ada)[:, :bsz]
    rope_c, rope_sa, rope_sb = _rope_tables(positions)
    e3 = _expand_matrix()

    for l in range(depth):
        lambda_init = 0.8 - 0.6 * math.exp(-0.3 * l)
        shift = mod[l, :, :d].reshape(bsz, 1, d)
        scale = mod[l, :, d:2 * d].reshape(bsz, 1, d)
        gate = mod[l, :, 2 * d:].reshape(bsz, 1, d)
        w_main, w_dt = _reorder_w_in(w_in[l])
        p, dt_raw = _in_proj(x2, seq, norm_w[l], scale, shift, w_main, w_dt,
                             rope_c, rope_sa, rope_sb)
        lam_prm = jnp.zeros((SUBLANES, LANES), jnp.float32)
        lam_prm = lam_prm.at[0:4, :ATT_HEAD_DIM].set(
            jnp.stack([lambda_q1[l], lambda_k1[l], lambda_q2[l], lambda_k2[l]]))
        o_att = _diff_attention(p, bsz, seq, lam_prm, attn_subln_w[l], lambda_init)
        y_ssm = _ssd_branch(p, dt_raw, bsz, seq, conv_w[l], conv_b[l], dt_bias[l], a_log[l],
                            d_skip[l], ssm_norm_w[l], e3)
        x2 = _merge_out(o_att, y_ssm, p, x2, seq, gate,
                        w_att_branch[l].astype(jnp.bfloat16), w_ssm_branch[l].astype(jnp.bfloat16),
                        w_out[l].astype(jnp.bfloat16),
                        final_norm_w if l == depth - 1 else None)
    return x2.reshape(bsz, seq, d)
```

```python
import functools
import math

import numpy as np
import jax
import jax.numpy as jnp
from jax import lax
from jax.experimental import pallas as pl
from jax.experimental.pallas import tpu as pltpu

D_MODEL = 1024
DEPTH = 2
ATT_HEADS = 4
ATT_HEAD_DIM = 64
ATT_V_DIM = 2 * ATT_HEAD_DIM
ATT_WIDTH = ATT_HEADS * ATT_V_DIM
ROPE_THETA = 500000.0
ROPE_DIM = ATT_HEAD_DIM // 4
SSM_HEAD_DIM = 64
SSM_WIDTH = D_MODEL
SSM_HEADS = SSM_WIDTH // SSM_HEAD_DIM
SSM_GROUPS = 2
SSM_STATE = 128
SSM_CONV = 4
SSM_CHUNK = 128
GROUP_WIDTH = SSM_WIDTH // SSM_GROUPS
HEADS_PER_GROUP = SSM_HEADS // SSM_GROUPS
EPS = 1e-5

LANES = 128
SUBLANES = 8
CHUNK_COLS = 512
CH_Q, CH_K, CH_V, CH_G = 0, 1, 2, 3
CH_Z = 4
CH_XS = 6
CH_BC = 8
CH_MGA = 9
CH_MGS = 11
N_CHUNKS = 13
LOG2_E = math.log2(math.e)
NEG_BIG = -1e30

VMEM_LIMIT = 48 * 1024 * 1024

TM_PROJ = 512
TQ = 512
ATT_ROWS = 256
TM_OUT = 512


def _cparams(sem):
    return pltpu.CompilerParams(dimension_semantics=sem, vmem_limit_bytes=VMEM_LIMIT)


def _sigmoid(x):
    return 1.0 / (1.0 + jnp.exp(-x))


def _silu(x):
    return x * _sigmoid(x)


def _split3(x):
    hi = x.astype(jnp.bfloat16)
    r1 = x - hi.astype(jnp.float32)
    mid = r1.astype(jnp.bfloat16)
    lo = (r1 - mid.astype(jnp.float32)).astype(jnp.bfloat16)
    return hi, mid, lo


def _ada_kernel(c_ref, w_ref, b_ref, o_ref):
    o_ref[...] = jnp.dot(c_ref[...], w_ref[...], preferred_element_type=jnp.float32,
                         precision=lax.Precision.HIGHEST) + b_ref[...]


def _ada_mod(c_pad, w_ada, b_ada):
    depth, d, d3 = w_ada.shape
    rows = c_pad.shape[0]
    nblk = d3 // d
    return pl.pallas_call(
        _ada_kernel,
        out_shape=jax.ShapeDtypeStruct((depth, rows, d3), jnp.float32),
        grid=(depth, nblk),
        in_specs=[pl.BlockSpec((rows, d), lambda l, n: (0, 0)),
                  pl.BlockSpec((None, d, d), lambda l, n: (l, 0, n)),
                  pl.BlockSpec((None, 1, d), lambda l, n: (l, 0, n))],
        out_specs=pl.BlockSpec((None, rows, d), lambda l, n: (l, 0, n)),
        compiler_params=_cparams(("arbitrary", "arbitrary")),
        name="ada_mod",
    )(c_pad, w_ada, b_ada.reshape(depth, 1, d3))


def _rope_kernel(pos_ref, freq_ref, ma_ref, mb_ref, c_ref, sa_ref, sb_ref):
    ang = pos_ref[...].astype(jnp.float32) * freq_ref[...]
    sn = jnp.sin(ang)
    c_ref[...] = jnp.cos(ang)
    sa_ref[...] = sn * ma_ref[...]
    sb_ref[...] = sn * mb_ref[...]


def _rope_tables(positions):
    t = positions.size
    half = ROPE_DIM // 2
    inv_freq = ROPE_THETA ** (-jnp.arange(0, ROPE_DIM, 2, dtype=jnp.float32) / ROPE_DIM)
    d = np.arange(LANES) % ATT_HEAD_DIM
    in_rope = d < ROPE_DIM
    freq = jnp.where(in_rope, inv_freq[d % half], 0.0).astype(jnp.float32).reshape(1, LANES)
    ma = jnp.asarray((in_rope & (d >= half)).astype(np.float32)).reshape(1, LANES)
    mb = jnp.asarray(-(d < half).astype(np.float32)).reshape(1, LANES)
    pos_b = jnp.broadcast_to(positions.reshape(t, 1), (t, LANES))
    tr = 1024
    row = pl.BlockSpec((1, LANES), lambda i: (0, 0))
    tab = pl.BlockSpec((tr, LANES), lambda i: (i, 0))
    sds = jax.ShapeDtypeStruct((t, LANES), jnp.float32)
    return pl.pallas_call(
        _rope_kernel, out_shape=(sds, sds, sds), grid=(t // tr,),
        in_specs=[tab, row, row, row], out_specs=(tab, tab, tab),
        compiler_params=_cparams(("arbitrary",)), name="rope_tables",
    )(pos_b, freq, ma, mb)


def _inproj_kernel(x_ref, nw_ref, sc_ref, sh_ref, w_ref, wdt_ref, c_ref, sa_ref, sb_ref,
                   p_ref, dt_ref, h_scr):
    x = x_ref[...]
    y = x * lax.rsqrt(jnp.mean(x * x, axis=-1, keepdims=True) + EPS) * nw_ref[...]
    h = y * (1.0 + sc_ref[...]) + sh_ref[...]
    h_scr[...] = h.astype(jnp.bfloat16)
    dt_ref[...] = jnp.dot(h_scr[...], wdt_ref[...], preferred_element_type=jnp.float32)

    def rope(t):
        reps = CHUNK_COLS // LANES
        c = jnp.tile(c_ref[...], (1, reps))
        sa = jnp.tile(sa_ref[...], (1, reps))
        sb = jnp.tile(sb_ref[...], (1, reps))
        half = ROPE_DIM // 2
        return (t * c + pltpu.roll(t, half, 1) * sa + pltpu.roll(t, CHUNK_COLS - half, 1) * sb)

    for n in range(N_CHUNKS):
        acc = jnp.dot(h_scr[...], w_ref[:, n * CHUNK_COLS:(n + 1) * CHUNK_COLS],
                      preferred_element_type=jnp.float32)
        if n == CH_Q:
            acc = rope(acc) * (ATT_HEAD_DIM ** -0.5 * LOG2_E)
        elif n == CH_K:
            acc = rope(acc)
        p_ref[n] = acc.astype(p_ref.dtype)


def _in_proj(x2, seq, norm_w, scale, shift, w_main, w_dt, rope_c, rope_sa, rope_sb):
    t, d = x2.shape
    tm = min(TM_PROJ, seq)
    per_b = seq // tm
    vec = lambda: pl.BlockSpec((None, 1, d), lambda m: (m // per_b, 0, 0))
    tab = lambda: pl.BlockSpec((tm, LANES), lambda m: (m, 0))
    once = lambda shape: pl.BlockSpec(shape, lambda m: (0, 0), pipeline_mode=pl.Buffered(1))
    return pl.pallas_call(
        _inproj_kernel,
        out_shape=(jax.ShapeDtypeStruct((N_CHUNKS, t, CHUNK_COLS), jnp.bfloat16),
                   jax.ShapeDtypeStruct((t, LANES), jnp.float32)),
        grid=(t // tm,),
        in_specs=[pl.BlockSpec((tm, d), lambda m: (m, 0)),
                  once((1, d)), vec(), vec(),
                  once((d, N_CHUNKS * CHUNK_COLS)), once((d, LANES)),
                  tab(), tab(), tab()],
        out_specs=(pl.BlockSpec((N_CHUNKS, tm, CHUNK_COLS), lambda m: (0, m, 0)),
                   pl.BlockSpec((tm, LANES), lambda m: (m, 0))),
        scratch_shapes=[pltpu.VMEM((tm, d), jnp.bfloat16)],
        compiler_params=_cparams(("arbitrary",)),
        name="in_proj",
    )(x2, norm_w.reshape(1, d), scale, shift, w_main, w_dt, rope_c, rope_sa, rope_sb)


def _attn_kernel(lam_ref, q_ref, k_ref, v_ref, g_ref, w_ref, o_ref, qs_scr, vx_scr, sa_scr, sb_scr,
                 m_scr, acc_scr, *, tq, lambda_init):
    qi = pl.program_id(2)

    @pl.when(qi == 0)
    def _():
        vx_scr[:, :LANES] = v_ref[...]
        vx_scr[:, LANES:] = jnp.ones((vx_scr.shape[0], LANES), vx_scr.dtype)

    q = q_ref[...]
    lane = lax.broadcasted_iota(jnp.int32, q.shape, 1)
    zero = jnp.zeros_like(q)
    qs_scr[0:tq, :] = jnp.where(lane < ATT_HEAD_DIM, q, zero)
    qs_scr[tq:, :] = jnp.where(lane >= ATT_HEAD_DIM, q, zero)
    m_scr[...] = jnp.full(m_scr.shape, NEG_BIG, jnp.float32)
    acc_scr[...] = jnp.zeros(acc_scr.shape, jnp.float32)

    n_chains = 2 * tq // ATT_ROWS

    def scores(j, dst):
        k = k_ref[pl.ds(pl.multiple_of(j * tq, tq), tq), :]
        for rc in range(n_chains):
            rows = slice(rc * ATT_ROWS, (rc + 1) * ATT_ROWS)
            dst[rows, :] = lax.dot_general(qs_scr[rows, :], k, (((1,), (1,)), ((), ())),
                                           preferred_element_type=jnp.float32)

    def consume(j, src, diagonal):
        start = pl.multiple_of(j * tq, tq)
        for rc in range(n_chains):
            rows = slice(rc * ATT_ROWS, (rc + 1) * ATT_ROWS)
            row0 = (rc * ATT_ROWS) % tq
            ncol = row0 + ATT_ROWS if diagonal else tq
            s = src[rows, :ncol]
            if diagonal:
                r = lax.broadcasted_iota(jnp.int32, s.shape, 0) + row0
                c = lax.broadcasted_iota(jnp.int32, s.shape, 1)
                s = jnp.where(c <= r, s, NEG_BIG)
            m_prev = m_scr[rows, :]
            m_new = jnp.maximum(m_prev, jnp.max(s, axis=-1, keepdims=True))
            alpha = jnp.exp2(m_prev - m_new)
            p = jnp.exp2(s - jnp.tile(m_new, (1, ncol // LANES)))
            pv = jnp.dot(p.astype(jnp.bfloat16), vx_scr[pl.ds(start, ncol), :],
                         preferred_element_type=jnp.float32)
            acc_scr[rows, :] = jnp.tile(alpha, (1, 2)) * acc_scr[rows, :] + pv
            m_scr[rows, :] = m_new

    scores(0, sa_scr)

    def body(i, carry):
        j = 2 * i
        scores(j + 1, sb_scr)
        consume(j, sa_scr, False)
        scores(j + 2, sa_scr)
        consume(j + 1, sb_scr, False)
        return carry

    lax.fori_loop(0, qi // 2, body, 0)

    @pl.when(qi % 2 == 0)
    def _():
        consume(qi, sa_scr, True)

    @pl.when(qi % 2 == 1)
    def _():
        scores(qi, sb_scr)
        consume(qi - 1, sa_scr, False)
        consume(qi, sb_scr, True)

    prm = lam_ref[...]
    lam = (jnp.exp(jnp.sum(prm[0:1] * prm[1:2], axis=-1, keepdims=True))
           - jnp.exp(jnp.sum(prm[2:3] * prm[3:4], axis=-1, keepdims=True)) + lambda_init)
    o_all = acc_scr[:, :LANES] / acc_scr[:, LANES:]
    o = o_all[:tq] - lam * o_all[tq:]
    o = o * lax.rsqrt(jnp.mean(o * o, axis=-1, keepdims=True) + EPS) * w_ref[...]
    o = o * (1.0 - lambda_init)
    o_ref[...] = (o * _silu(g_ref[...].astype(jnp.float32))).astype(o_ref.dtype)


def _diff_attention(p, bsz, seq, lam_prm, subln_w, lambda_init):
    t = bsz * seq
    tq = min(TQ, seq)
    nq = seq // tq
    blk = lambda ch: pl.BlockSpec((None, tq, LANES), lambda b, h, i: (ch, b * nq + i, h))
    full = lambda ch: pl.BlockSpec((None, seq, LANES), lambda b, h, i: (ch, b, h))
    return pl.pallas_call(
        functools.partial(_attn_kernel, tq=tq, lambda_init=lambda_init),
        out_shape=jax.ShapeDtypeStruct((t, ATT_WIDTH), jnp.bfloat16),
        grid=(bsz, ATT_HEADS, nq),
        in_specs=[pl.BlockSpec((SUBLANES, LANES), lambda b, h, i: (0, 0)),
                  blk(CH_Q), full(CH_K), full(CH_V), blk(CH_G),
                  pl.BlockSpec((1, LANES), lambda b, h, i: (0, 0))],
        out_specs=pl.BlockSpec((tq, LANES), lambda b, h, i: (b * nq + i, h)),
        scratch_shapes=[pltpu.VMEM((2 * tq, LANES), jnp.bfloat16),
                        pltpu.VMEM((seq, 2 * LANES), jnp.bfloat16),
                        pltpu.VMEM((2 * tq, tq), jnp.float32),
                        pltpu.VMEM((2 * tq, tq), jnp.float32),
                        pltpu.VMEM((2 * tq, LANES), jnp.float32),
                        pltpu.VMEM((2 * tq, 2 * LANES), jnp.float32)],
        compiler_params=_cparams(("arbitrary", "arbitrary", "arbitrary")),
        name="diff_attn",
    )(lam_prm, p, p, p, p, subln_w.reshape(1, LANES))


def _ssd_kernel(xs0_ref, xs1_ref, bc_ref, z0_ref, z1_ref, dt_ref, cw_ref, cb_ref, dtb_ref,
                alog_ref, dsk_ref, nw_ref, e2_ref, shift_ref, o_ref, ext_scr, h_scr):
    L = SSM_CHUNK
    conv_ch = 3 * CHUNK_COLS

    chunk = pl.program_id(1)
    parity = chunk % 2

    @pl.when(chunk == 0)
    def _():
        ext_scr[L:, :] = jnp.zeros((L, conv_ch), ext_scr.dtype)
        h_scr[...] = jnp.zeros(h_scr.shape, jnp.float32)

    cur_rows = pl.ds(pl.multiple_of(parity * L, L), L)
    cur = jnp.concatenate([xs0_ref[...], xs1_ref[...], bc_ref[...]], axis=1)
    ext_scr[cur_rows, :] = cur
    shifted = jnp.dot(shift_ref[parity], ext_scr[...], preferred_element_type=jnp.float32)
    conv = cb_ref[...] + cur.astype(jnp.float32) * cw_ref[SSM_CONV - 1:SSM_CONV, :]
    for tap in range(SSM_CONV - 1):
        conv = conv + shifted[tap * L:(tap + 1) * L, :] * cw_ref[tap:tap + 1, :]
    act = _silu(conv)
    xs = act[:, :SSM_WIDTH]
    b_mat = act[:, SSM_WIDTH:SSM_WIDTH + SSM_GROUPS * SSM_STATE]
    c_mat = act[:, SSM_WIDTH + SSM_GROUPS * SSM_STATE:]

    dtr = dt_ref[...] + dtb_ref[...]
    dt = jnp.maximum(dtr, 0.0) + jnp.log1p(jnp.exp(-jnp.abs(dtr)))
    adt = dt * (-jnp.exp(alog_ref[...]))
    r_i = lax.broadcasted_iota(jnp.int32, (L, L), 0)
    c_i = lax.broadcasted_iota(jnp.int32, (L, L), 1)
    causal = c_i <= r_i
    tril = jnp.where(causal, 1.0, 0.0).astype(jnp.bfloat16)
    hi, mid, lo = _split3(adt)
    acs = jnp.dot(jnp.concatenate([tril, tril, tril], axis=1),
                  jnp.concatenate([hi, mid, lo], axis=0),
                  preferred_element_type=jnp.float32)
    acs_t = acs.T

    both = jnp.concatenate([dt, acs], axis=0)
    hi = both.astype(jnp.bfloat16)
    lo = (both - hi.astype(jnp.float32)).astype(jnp.bfloat16)
    ex = jnp.dot(jnp.concatenate([hi, lo], axis=1), e2_ref[...],
                 preferred_element_type=jnp.float32)
    dt_e = ex[:L]
    acs_e = ex[L:]
    acs_last = acs_e[L - 1:L, :]
    eacs_e = jnp.exp(acs_e)
    xd = xs * dt_e
    xdd = (xd * jnp.exp(acs_last - acs_e)).astype(jnp.bfloat16)
    chunk_decay = jnp.exp(acs_last)

    lane = lax.broadcasted_iota(jnp.int32, (L, LANES), 1)
    b_t = b_mat.T
    y_groups = []
    for g in range(SSM_GROUPS):
        gs = slice(g * GROUP_WIDTH, (g + 1) * GROUP_WIDTH)
        c_g = c_mat[:, g * SSM_STATE:(g + 1) * SSM_STATE].astype(jnp.bfloat16)
        bt_g = b_t[g * SSM_STATE:(g + 1) * SSM_STATE, :].astype(jnp.bfloat16)
        cb = jnp.dot(c_g, bt_g, preferred_element_type=jnp.float32)

        h_g = h_scr[:, gs]
        y_off = jnp.dot(c_g, h_g.astype(jnp.bfloat16),
                        preferred_element_type=jnp.float32) * eacs_e[:, gs]
        st = jnp.dot(bt_g, xdd[:, gs], preferred_element_type=jnp.float32)
        h_scr[:, gs] = h_g * chunk_decay[:, gs] + st

        parts = []
        for pr in range(HEADS_PER_GROUP // 2):
            col0 = g * GROUP_WIDTH + pr * LANES
            xd_pair = xd[:, col0:col0 + LANES]
            ms = []
            for hh in range(2):
                j = g * HEADS_PER_GROUP + 2 * pr + hh
                seg = acs[:, j:j + 1] - acs_t[j:j + 1, :]
                lm = jnp.exp(jnp.where(causal, seg, NEG_BIG))
                ms.append((cb * lm).astype(jnp.bfloat16))
            w_pair = jnp.concatenate(
                [jnp.where(lane < SSM_HEAD_DIM, xd_pair, 0.0),
                 jnp.where(lane >= SSM_HEAD_DIM, xd_pair, 0.0)], axis=0).astype(jnp.bfloat16)
            parts.append(jnp.dot(jnp.concatenate(ms, axis=1), w_pair,
                                 preferred_element_type=jnp.float32))
        y_diag = jnp.concatenate(parts, axis=1)

        y = y_diag + y_off + dsk_ref[:, gs] * xs[:, gs]
        zg = (z0_ref if g == 0 else z1_ref)[...].astype(jnp.float32)
        y = y * _silu(zg)
        y = y * lax.rsqrt(jnp.mean(y * y, axis=-1, keepdims=True) + EPS) * nw_ref[:, gs]
        y_groups.append(y)
    o_ref[...] = jnp.concatenate(y_groups, axis=1).astype(o_ref.dtype)


def _ssd_branch(p, dt_raw, bsz, seq, conv_w, conv_b, dt_bias, a_log, d_skip, ssm_norm_w):
    t = bsz * seq
    L = SSM_CHUNK
    nc = seq // L
    conv_ch = 3 * CHUNK_COLS
    blk = lambda ch: pl.BlockSpec((None, L, CHUNK_COLS), lambda b, c: (ch, b * nc + c, 0))
    row = lambda w: pl.BlockSpec((1, w), lambda b, c: (0, 0))
    pad16 = lambda v: jnp.pad(v.astype(jnp.float32), (0, LANES - SSM_HEADS)).reshape(1, LANES)
    return pl.pallas_call(
        _ssd_kernel,
        out_shape=jax.ShapeDtypeStruct((t, SSM_WIDTH), jnp.bfloat16),
        grid=(bsz, nc),
        in_specs=[blk(CH_XS), blk(CH_XS + 1), blk(CH_BC), blk(CH_Z), blk(CH_Z + 1),
                  pl.BlockSpec((L, LANES), lambda b, c: (b * nc + c, 0)),
                  pl.BlockSpec((SSM_CONV, conv_ch), lambda b, c: (0, 0)),
                  row(conv_ch), row(LANES), row(LANES), row(SSM_WIDTH), row(SSM_WIDTH),
                  pl.BlockSpec((2 * LANES, SSM_WIDTH), lambda b, c: (0, 0)),
                  pl.BlockSpec((2, (SSM_CONV - 1) * L, 2 * L), lambda b, c: (0, 0, 0))],
        out_specs=pl.BlockSpec((L, SSM_WIDTH), lambda b, c: (b * nc + c, 0)),
        scratch_shapes=[pltpu.VMEM((2 * L, conv_ch), jnp.bfloat16),
                        pltpu.VMEM((SSM_STATE, SSM_WIDTH), jnp.float32)],
        compiler_params=_cparams(("arbitrary", "arbitrary")),
        name="ssd_branch",
    )(p, p, p, p, p, dt_raw, conv_w, conv_b.reshape(1, conv_ch), pad16(dt_bias), pad16(a_log),
      jnp.repeat(d_skip.astype(jnp.float32), SSM_HEAD_DIM).reshape(1, SSM_WIDTH),
      ssm_norm_w.reshape(1, SSM_WIDTH), _expand_matrix(), _conv_shift_matrix())


def _merge_kernel(*refs, final):
    if final:
        (oa_ref, ys_ref, ga0_ref, ga1_ref, gs0_ref, gs1_ref, x_ref, gate_ref,
         wa_ref, ws_ref, wo_ref, fw_ref, o_ref) = refs
    else:
        (oa_ref, ys_ref, ga0_ref, ga1_ref, gs0_ref, gs1_ref, x_ref, gate_ref,
         wa_ref, ws_ref, wo_ref, o_ref) = refs
    y_att = jnp.dot(oa_ref[...], wa_ref[...], preferred_element_type=jnp.float32)
    y_ssm = jnp.dot(ys_ref[...], ws_ref[...], preferred_element_type=jnp.float32)
    mg_att = jnp.concatenate([ga0_ref[...], ga1_ref[...]], axis=1).astype(jnp.float32)
    mg_ssm = jnp.concatenate([gs0_ref[...], gs1_ref[...]], axis=1).astype(jnp.float32)
    merged = _sigmoid(mg_att) * y_att + _sigmoid(mg_ssm) * y_ssm
    out = x_ref[...] + gate_ref[...] * jnp.dot(merged.astype(jnp.bfloat16), wo_ref[...],
                                                preferred_element_type=jnp.float32)
    if final:
        out = out * lax.rsqrt(jnp.mean(out * out, axis=-1, keepdims=True) + EPS) * fw_ref[...]
    o_ref[...] = out


def _merge_out(o_att, y_ssm, p, x2, seq, gate, w_att, w_ssm, w_out, final_w):
    t, d = x2.shape
    tm = min(TM_OUT, seq)
    per_b = seq // tm
    final = final_w is not None
    chunk = lambda ch: pl.BlockSpec((None, tm, CHUNK_COLS), lambda m: (ch, m, 0))
    res = lambda shape: pl.BlockSpec(shape, lambda m: (0, 0))
    in_specs = [pl.BlockSpec((tm, ATT_WIDTH), lambda m: (m, 0)),
                pl.BlockSpec((tm, SSM_WIDTH), lambda m: (m, 0)),
                chunk(CH_MGA), chunk(CH_MGA + 1), chunk(CH_MGS), chunk(CH_MGS + 1),
                pl.BlockSpec((tm, d), lambda m: (m, 0)),
                pl.BlockSpec((None, 1, d), lambda m: (m // per_b, 0, 0)),
                res((ATT_WIDTH, d)), res((SSM_WIDTH, d)), res((d, d))]
    args = [o_att, y_ssm, p, p, p, p, x2, gate, w_att, w_ssm, w_out]
    if final:
        in_specs.append(res((1, d)))
        args.append(final_w.reshape(1, d))
    return pl.pallas_call(
        functools.partial(_merge_kernel, final=final),
        out_shape=jax.ShapeDtypeStruct((t, d), jnp.float32),
        grid=(t // tm,),
        in_specs=in_specs,
        out_specs=pl.BlockSpec((tm, d), lambda m: (m, 0)),
        compiler_params=_cparams(("arbitrary",)),
        name="merge_out",
    )(*args)


def _reorder_w_in(w_in_l):
    sizes = [ATT_WIDTH, ATT_WIDTH, ATT_WIDTH, ATT_WIDTH, SSM_WIDTH,
             SSM_WIDTH + 2 * SSM_GROUPS * SSM_STATE, SSM_HEADS, D_MODEL, D_MODEL]
    offs = np.cumsum([0] + sizes)
    seg = lambda i: w_in_l[:, offs[i]:offs[i + 1]]
    q, k, v, g, z, xbc, dtc, mga, mgs = [seg(i) for i in range(9)]
    w_main = jnp.concatenate([q, k, v, g, z, xbc, mga, mgs], axis=1).astype(jnp.bfloat16)
    w_dt = jnp.pad(dtc, ((0, 0), (0, LANES - SSM_HEADS))).astype(jnp.bfloat16)
    return w_main, w_dt


def _expand_matrix():
    e = np.zeros((LANES, SSM_WIDTH), np.float32)
    for j in range(SSM_HEADS):
        e[j, j * SSM_HEAD_DIM:(j + 1) * SSM_HEAD_DIM] = 1.0
    return jnp.asarray(np.concatenate([e, e], axis=0), dtype=jnp.bfloat16)


def _conv_shift_matrix():
    L = SSM_CHUNK
    s = np.zeros((2, (SSM_CONV - 1) * L, 2 * L), np.float32)
    for parity in range(2):
        for tap in range(SSM_CONV - 1):
            for l in range(L):
                r = l - (SSM_CONV - 1 - tap)
                col = parity * L + r if r >= 0 else (1 - parity) * L + L + r
                s[parity, tap * L + l, col] = 1.0
    return jnp.asarray(s, dtype=jnp.bfloat16)


def kernel(x, c, positions, w_ada, b_ada, norm_w, w_in, lambda_q1, lambda_k1, lambda_q2, lambda_k2,
           attn_subln_w, w_att_branch, conv_w, conv_b, dt_bias, a_log, d_skip, ssm_norm_w,
           w_ssm_branch, w_out, final_norm_w):
    bsz, seq, d = x.shape
    depth = w_ada.shape[0]
    t = bsz * seq
    x2 = x.reshape(t, d)

    c_pad = jnp.pad(c, ((0, SUBLANES - bsz % SUBLANES), (0, 0))) if bsz % SUBLANES else c
    mod = _ada_mod(c_pad, w_ada, b_ada)[:, :bsz]
    rope_c, rope_sa, rope_sb = _rope_tables(positions)

    for l in range(depth):
        lambda_init = 0.8 - 0.6 * math.exp(-0.3 * l)
        shift = mod[l, :, :d].reshape(bsz, 1, d)
        scale = mod[l, :, d:2 * d].reshape(bsz, 1, d)
        gate = mod[l, :, 2 * d:].reshape(bsz, 1, d)
        w_main, w_dt = _reorder_w_in(w_in[l])
        p, dt_raw = _in_proj(x2, seq, norm_w[l], scale, shift, w_main, w_dt,
                             rope_c, rope_sa, rope_sb)
        lam_prm = jnp.zeros((SUBLANES, LANES), jnp.float32)
        lam_prm = lam_prm.at[0:4, :ATT_HEAD_DIM].set(
            jnp.stack([lambda_q1[l], lambda_k1[l], lambda_q2[l], lambda_k2[l]]))
        o_att = _diff_attention(p, bsz, seq, lam_prm, attn_subln_w[l], lambda_init)
        y_ssm = _ssd_branch(p, dt_raw, bsz, seq, conv_w[l], conv_b[l], dt_bias[l], a_log[l],
                            d_skip[l], ssm_norm_w[l])
        x2 = _merge_out(o_att, y_ssm, p, x2, seq, gate,
                        w_att_branch[l].astype(jnp.bfloat16), w_ssm_branch[l].astype(jnp.bfloat16),
                        w_out[l].astype(jnp.bfloat16),
                        final_norm_w if l == depth - 1 else None)
    return x2.reshape(bsz, seq, d)
```

```python
import functools
import math

import numpy as np
import jax
import jax.numpy as jnp
from jax import lax
from jax.experimental import pallas as pl
from jax.experimental.pallas import tpu as pltpu

D_MODEL = 1024
DEPTH = 2
ATT_HEADS = 4
ATT_HEAD_DIM = 64
ATT_V_DIM = 2 * ATT_HEAD_DIM
ATT_WIDTH = ATT_HEADS * ATT_V_DIM
ROPE_THETA = 500000.0
ROPE_DIM = ATT_HEAD_DIM // 4
SSM_HEAD_DIM = 64
SSM_WIDTH = D_MODEL
SSM_HEADS = SSM_WIDTH // SSM_HEAD_DIM
SSM_GROUPS = 2
SSM_STATE = 128
SSM_CONV = 4
SSM_CHUNK = 128
GROUP_WIDTH = SSM_WIDTH // SSM_GROUPS
HEADS_PER_GROUP = SSM_HEADS // SSM_GROUPS
EPS = 1e-5

LANES = 128
SUBLANES = 8
CHUNK_COLS = 512
CH_Q, CH_K, CH_V, CH_G = 0, 1, 2, 3
CH_Z = 4
CH_XS = 6
CH_BC = 8
CH_MGA = 9
CH_MGS = 11
N_CHUNKS = 13
LOG2_E = math.log2(math.e)
NEG_BIG = -1e30

VMEM_LIMIT = 48 * 1024 * 1024

TM_PROJ = 512
TQ = 512
ATT_ROWS = 256
TM_OUT = 512
SSD_SUB = 2


def _cparams(sem):
    return pltpu.CompilerParams(dimension_semantics=sem, vmem_limit_bytes=VMEM_LIMIT)


def _sigmoid(x):
    return 0.5 + 0.5 * jnp.tanh(0.5 * x)


def _silu(x):
    h = 0.5 * x
    return h + h * jnp.tanh(h)


def _split3(x):
    hi = x.astype(jnp.bfloat16)
    r1 = x - hi.astype(jnp.float32)
    mid = r1.astype(jnp.bfloat16)
    lo = (r1 - mid.astype(jnp.float32)).astype(jnp.bfloat16)
    return hi, mid, lo


def _ada_kernel(c_ref, w_ref, b_ref, o_ref):
    o_ref[...] = jnp.dot(c_ref[...], w_ref[...], preferred_element_type=jnp.float32,
                         precision=lax.Precision.HIGHEST) + b_ref[...]


def _ada_mod(c_pad, w_ada, b_ada):
    depth, d, d3 = w_ada.shape
    rows = c_pad.shape[0]
    nblk = d3 // d
    return pl.pallas_call(
        _ada_kernel,
        out_shape=jax.ShapeDtypeStruct((depth, rows, d3), jnp.float32),
        grid=(depth, nblk),
        in_specs=[pl.BlockSpec((rows, d), lambda l, n: (0, 0)),
                  pl.BlockSpec((None, d, d), lambda l, n: (l, 0, n)),
                  pl.BlockSpec((None, 1, d), lambda l, n: (l, 0, n))],
        out_specs=pl.BlockSpec((None, rows, d), lambda l, n: (l, 0, n)),
        compiler_params=_cparams(("arbitrary", "arbitrary")),
        name="ada_mod",
    )(c_pad, w_ada, b_ada.reshape(depth, 1, d3))


def _rope_kernel(pos_ref, freq_ref, ma_ref, mb_ref, c_ref, sa_ref, sb_ref):
    ang = pos_ref[...].astype(jnp.float32) * freq_ref[...]
    sn = jnp.sin(ang)
    c_ref[...] = jnp.cos(ang)
    sa_ref[...] = sn * ma_ref[...]
    sb_ref[...] = sn * mb_ref[...]


def _rope_tables(positions):
    t = positions.size
    half = ROPE_DIM // 2
    inv_freq = ROPE_THETA ** (-jnp.arange(0, ROPE_DIM, 2, dtype=jnp.float32) / ROPE_DIM)
    d = np.arange(LANES) % ATT_HEAD_DIM
    in_rope = d < ROPE_DIM
    freq = jnp.where(in_rope, inv_freq[d % half], 0.0).astype(jnp.float32).reshape(1, LANES)
    ma = jnp.asarray((in_rope & (d >= half)).astype(np.float32)).reshape(1, LANES)
    mb = jnp.asarray(-(d < half).astype(np.float32)).reshape(1, LANES)
    pos_b = jnp.broadcast_to(positions.reshape(t, 1), (t, LANES))
    tr = 1024
    row = pl.BlockSpec((1, LANES), lambda i: (0, 0))
    tab = pl.BlockSpec((tr, LANES), lambda i: (i, 0))
    sds = jax.ShapeDtypeStruct((t, LANES), jnp.float32)
    return pl.pallas_call(
        _rope_kernel, out_shape=(sds, sds, sds), grid=(t // tr,),
        in_specs=[tab, row, row, row], out_specs=(tab, tab, tab),
        compiler_params=_cparams(("arbitrary",)), name="rope_tables",
    )(pos_b, freq, ma, mb)


def _inproj_kernel(x_ref, nw_ref, sc_ref, sh_ref, w_ref, wdt_ref, c_ref, sa_ref, sb_ref,
                   p_ref, dt_ref, h_scr):
    x = x_ref[...]
    y = x * lax.rsqrt(jnp.mean(x * x, axis=-1, keepdims=True) + EPS) * nw_ref[...]
    h = y * (1.0 + sc_ref[...]) + sh_ref[...]
    h_scr[...] = h.astype(jnp.bfloat16)
    dt_ref[...] = jnp.dot(h_scr[...], wdt_ref[...], preferred_element_type=jnp.float32)

    def rope(t):
        reps = CHUNK_COLS // LANES
        c = jnp.tile(c_ref[...], (1, reps))
        sa = jnp.tile(sa_ref[...], (1, reps))
        sb = jnp.tile(sb_ref[...], (1, reps))
        half = ROPE_DIM // 2
        return (t * c + pltpu.roll(t, half, 1) * sa + pltpu.roll(t, CHUNK_COLS - half, 1) * sb)

    for n in range(N_CHUNKS):
        acc = jnp.dot(h_scr[...], w_ref[:, n * CHUNK_COLS:(n + 1) * CHUNK_COLS],
                      preferred_element_type=jnp.float32)
        if n == CH_Q:
            acc = rope(acc) * (ATT_HEAD_DIM ** -0.5 * LOG2_E)
        elif n == CH_K:
            acc = rope(acc)
        p_ref[n] = acc.astype(p_ref.dtype)


def _in_proj(x2, seq, norm_w, scale, shift, w_main, w_dt, rope_c, rope_sa, rope_sb):
    t, d = x2.shape
    tm = min(TM_PROJ, seq)
    per_b = seq // tm
    vec = lambda: pl.BlockSpec((None, 1, d), lambda m: (m // per_b, 0, 0))
    tab = lambda: pl.BlockSpec((tm, LANES), lambda m: (m, 0))
    once = lambda shape: pl.BlockSpec(shape, lambda m: (0, 0), pipeline_mode=pl.Buffered(1))
    return pl.pallas_call(
        _inproj_kernel,
        out_shape=(jax.ShapeDtypeStruct((N_CHUNKS, t, CHUNK_COLS), jnp.bfloat16),
                   jax.ShapeDtypeStruct((t, LANES), jnp.float32)),
        grid=(t // tm,),
        in_specs=[pl.BlockSpec((tm, d), lambda m: (m, 0)),
                  once((1, d)), vec(), vec(),
                  once((d, N_CHUNKS * CHUNK_COLS)), once((d, LANES)),
                  tab(), tab(), tab()],
        out_specs=(pl.BlockSpec((N_CHUNKS, tm, CHUNK_COLS), lambda m: (0, m, 0)),
                   pl.BlockSpec((tm, LANES), lambda m: (m, 0))),
        scratch_shapes=[pltpu.VMEM((tm, d), jnp.bfloat16)],
        compiler_params=_cparams(("arbitrary",)),
        name="in_proj",
    )(x2, norm_w.reshape(1, d), scale, shift, w_main, w_dt, rope_c, rope_sa, rope_sb)


def _attn_kernel(lam_ref, q_ref, k_ref, v_ref, g_ref, w_ref, o_ref, qs_scr, vx_scr, sa_scr, sb_scr,
                 m_scr, acc_scr, *, tq, lambda_init):
    qi = pl.program_id(2)

    @pl.when(qi == 0)
    def _():
        vx_scr[:, :LANES] = v_ref[...]
        vx_scr[:, LANES:] = jnp.ones((vx_scr.shape[0], LANES), vx_scr.dtype)

    q = q_ref[...]
    lane = lax.broadcasted_iota(jnp.int32, q.shape, 1)
    zero = jnp.zeros_like(q)
    qs_scr[0:tq, :] = jnp.where(lane < ATT_HEAD_DIM, q, zero)
    qs_scr[tq:, :] = jnp.where(lane >= ATT_HEAD_DIM, q, zero)
    m_scr[...] = jnp.full(m_scr.shape, NEG_BIG, jnp.float32)
    acc_scr[...] = jnp.zeros(acc_scr.shape, jnp.float32)

    n_chains = 2 * tq // ATT_ROWS

    def scores(j, dst):
        k = k_ref[pl.ds(pl.multiple_of(j * tq, tq), tq), :]
        for rc in range(n_chains):
            rows = slice(rc * ATT_ROWS, (rc + 1) * ATT_ROWS)
            dst[rows, :] = lax.dot_general(qs_scr[rows, :], k, (((1,), (1,)), ((), ())),
                                           preferred_element_type=jnp.float32)

    def consume(j, src, diagonal):
        start = pl.multiple_of(j * tq, tq)
        for rc in range(n_chains):
            rows = slice(rc * ATT_ROWS, (rc + 1) * ATT_ROWS)
            row0 = (rc * ATT_ROWS) % tq
            ncol = row0 + ATT_ROWS if diagonal else tq
            s = src[rows, :ncol]
            if diagonal:
                r = lax.broadcasted_iota(jnp.int32, s.shape, 0) + row0
                c = lax.broadcasted_iota(jnp.int32, s.shape, 1)
                s = jnp.where(c <= r, s, NEG_BIG)
            m_prev = m_scr[rows, :]
            m_new = jnp.maximum(m_prev, jnp.max(s, axis=-1, keepdims=True))
            alpha = jnp.exp2(m_prev - m_new)
            p = jnp.exp2(s - jnp.tile(m_new, (1, ncol // LANES)))
            pv = jnp.dot(p.astype(jnp.bfloat16), vx_scr[pl.ds(start, ncol), :],
                         preferred_element_type=jnp.float32)
            acc_scr[rows, :] = jnp.tile(alpha, (1, 2)) * acc_scr[rows, :] + pv
            m_scr[rows, :] = m_new

    scores(0, sa_scr)

    def pair(j):
        scores(j + 1, sb_scr)
        consume(j, sa_scr, False)
        scores(j + 2, sa_scr)
        consume(j + 1, sb_scr, False)

    def body(i, carry):
        pair(4 * i)
        pair(4 * i + 2)
        return carry

    n_quads = qi // 4
    lax.fori_loop(0, n_quads, body, 0)

    @pl.when(qi % 4 >= 2)
    def _():
        pair(4 * n_quads)

    @pl.when(qi % 2 == 0)
    def _():
        consume(qi, sa_scr, True)

    @pl.when(qi % 2 == 1)
    def _():
        scores(qi, sb_scr)
        consume(qi - 1, sa_scr, False)
        consume(qi, sb_scr, True)

    prm = lam_ref[...]
    lam = (jnp.exp(jnp.sum(prm[0:1] * prm[1:2], axis=-1, keepdims=True))
           - jnp.exp(jnp.sum(prm[2:3] * prm[3:4], axis=-1, keepdims=True)) + lambda_init)
    o_all = acc_scr[:, :LANES] / acc_scr[:, LANES:]
    o = o_all[:tq] - lam * o_all[tq:]
    o = o * lax.rsqrt(jnp.mean(o * o, axis=-1, keepdims=True) + EPS) * w_ref[...]
    o = o * (1.0 - lambda_init)
    o_ref[...] = (o * _silu(g_ref[...].astype(jnp.float32))).astype(o_ref.dtype)


def _diff_attention(p, bsz, seq, lam_prm, subln_w, lambda_init):
    t = bsz * seq
    tq = min(TQ, seq)
    nq = seq // tq
    blk = lambda ch: pl.BlockSpec((None, tq, LANES), lambda b, h, i: (ch, b * nq + i, h))
    full = lambda ch: pl.BlockSpec((None, seq, LANES), lambda b, h, i: (ch, b, h))
    return pl.pallas_call(
        functools.partial(_attn_kernel, tq=tq, lambda_init=lambda_init),
        out_shape=jax.ShapeDtypeStruct((t, ATT_WIDTH), jnp.bfloat16),
        grid=(bsz, ATT_HEADS, nq),
        in_specs=[pl.BlockSpec((SUBLANES, LANES), lambda b, h, i: (0, 0)),
                  blk(CH_Q), full(CH_K), full(CH_V), blk(CH_G),
                  pl.BlockSpec((1, LANES), lambda b, h, i: (0, 0))],
        out_specs=pl.BlockSpec((tq, LANES), lambda b, h, i: (b * nq + i, h)),
        scratch_shapes=[pltpu.VMEM((2 * tq, LANES), jnp.bfloat16),
                        pltpu.VMEM((seq, 2 * LANES), jnp.bfloat16),
                        pltpu.VMEM((2 * tq, tq), jnp.float32),
                        pltpu.VMEM((2 * tq, tq), jnp.float32),
                        pltpu.VMEM((2 * tq, LANES), jnp.float32),
                        pltpu.VMEM((2 * tq, 2 * LANES), jnp.float32)],
        compiler_params=_cparams(("arbitrary", "arbitrary", "arbitrary")),
        name="diff_attn",
    )(lam_prm, p, p, p, p, subln_w.reshape(1, LANES))


def _ssd_kernel(xs0_ref, xs1_ref, bc_ref, z0_ref, z1_ref, dt_ref, cw_ref, cb_ref, dtb_ref,
                alog_ref, dsk_ref, nw_ref, e2_ref, shift_ref, o_ref, carry_scr, h_scr):
    L = SSM_CHUNK
    conv_ch = 3 * CHUNK_COLS
    step = pl.program_id(1)
    parity = step % 2

    @pl.when(step == 0)
    def _():
        carry_scr[1] = jnp.zeros((L, conv_ch), carry_scr.dtype)
        h_scr[...] = jnp.zeros(h_scr.shape, jnp.float32)

    r_i = lax.broadcasted_iota(jnp.int32, (L, L), 0)
    c_i = lax.broadcasted_iota(jnp.int32, (L, L), 1)
    causal = c_i <= r_i
    tril = jnp.where(causal, 1.0, 0.0).astype(jnp.bfloat16)
    tril3 = jnp.concatenate([tril, tril, tril], axis=1)
    lane = lax.broadcasted_iota(jnp.int32, (L, LANES), 1)
    neg_a = -jnp.exp(alog_ref[...])

    prev = carry_scr[1 - parity]
    for sub in range(SSD_SUB):
        rows = slice(sub * L, (sub + 1) * L)

        cur = jnp.concatenate([xs0_ref[rows, :], xs1_ref[rows, :], bc_ref[rows, :]], axis=1)
        shifted = jnp.dot(shift_ref[...], jnp.concatenate([prev, cur], axis=0),
                          preferred_element_type=jnp.float32)
        conv = cb_ref[...] + cur.astype(jnp.float32) * cw_ref[SSM_CONV - 1:SSM_CONV, :]
        for tap in range(SSM_CONV - 1):
            conv = conv + shifted[tap * L:(tap + 1) * L, :] * cw_ref[tap:tap + 1, :]
        prev = cur
        act = _silu(conv)
        xs = act[:, :SSM_WIDTH]
        b_mat = act[:, SSM_WIDTH:SSM_WIDTH + SSM_GROUPS * SSM_STATE]
        c_mat = act[:, SSM_WIDTH + SSM_GROUPS * SSM_STATE:]

        dtr = dt_ref[rows, :] + dtb_ref[...]
        dt = jnp.maximum(dtr, 0.0) + jnp.log1p(jnp.exp(-jnp.abs(dtr)))
        hi, mid, lo = _split3(dt * neg_a)
        acs = jnp.dot(tril3, jnp.concatenate([hi, mid, lo], axis=0),
                      preferred_element_type=jnp.float32) * LOG2_E
        acs_t = acs.T

        both = jnp.concatenate([dt, acs], axis=0)
        hi = both.astype(jnp.bfloat16)
        lo = (both - hi.astype(jnp.float32)).astype(jnp.bfloat16)
        ex = jnp.dot(jnp.concatenate([hi, lo], axis=1), e2_ref[...],
                     preferred_element_type=jnp.float32)
        dt_e = ex[:L]
        acs_e = ex[L:]
        acs_last = acs_e[L - 1:L, :]
        eacs_e = jnp.exp2(acs_e)
        xd = xs * dt_e
        xdd = (xd * jnp.exp2(acs_last - acs_e)).astype(jnp.bfloat16)
        chunk_decay = jnp.exp2(acs_last)

        b_t = b_mat.T
        for g in range(SSM_GROUPS):
            gs = slice(g * GROUP_WIDTH, (g + 1) * GROUP_WIDTH)
            c_g = c_mat[:, g * SSM_STATE:(g + 1) * SSM_STATE].astype(jnp.bfloat16)
            bt_g = b_t[g * SSM_STATE:(g + 1) * SSM_STATE, :].astype(jnp.bfloat16)
            cb = jnp.dot(c_g, bt_g, preferred_element_type=jnp.float32)

            h_g = h_scr[:, gs]
            y_off = jnp.dot(c_g, h_g.astype(jnp.bfloat16),
                            preferred_element_type=jnp.float32) * eacs_e[:, gs]
            st = jnp.dot(bt_g, xdd[:, gs], preferred_element_type=jnp.float32)
            h_scr[:, gs] = h_g * chunk_decay[:, gs] + st

            parts = []
            for pr in range(HEADS_PER_GROUP // 2):
                col0 = g * GROUP_WIDTH + pr * LANES
                xd_pair = xd[:, col0:col0 + LANES]
                ms = []
                for hh in range(2):
                    j = g * HEADS_PER_GROUP + 2 * pr + hh
                    seg = acs[:, j:j + 1] - acs_t[j:j + 1, :]
                    lm = jnp.exp2(jnp.where(causal, seg, NEG_BIG))
                    ms.append((cb * lm).astype(jnp.bfloat16))
                w_pair = jnp.concatenate(
                    [jnp.where(lane < SSM_HEAD_DIM, xd_pair, 0.0),
                     jnp.where(lane >= SSM_HEAD_DIM, xd_pair, 0.0)], axis=0).astype(jnp.bfloat16)
                parts.append(jnp.dot(jnp.concatenate(ms, axis=1), w_pair,
                                     preferred_element_type=jnp.float32))
            y_diag = jnp.concatenate(parts, axis=1)

            y = y_diag + y_off + dsk_ref[:, gs] * xs[:, gs]
            zg = (z0_ref if g == 0 else z1_ref)[rows, :].astype(jnp.float32)
            y = y * _silu(zg)
            y = y * lax.rsqrt(jnp.mean(y * y, axis=-1, keepdims=True) + EPS) * nw_ref[:, gs]
            o_ref[rows, gs] = y.astype(o_ref.dtype)
    carry_scr[parity] = prev


def _ssd_branch(p, dt_raw, bsz, seq, conv_w, conv_b, dt_bias, a_log, d_skip, ssm_norm_w):
    t = bsz * seq
    L = SSM_CHUNK
    rows = SSD_SUB * L
    ns = seq // rows
    conv_ch = 3 * CHUNK_COLS
    blk = lambda ch: pl.BlockSpec((None, rows, CHUNK_COLS), lambda b, c: (ch, b * ns + c, 0))
    row = lambda w: pl.BlockSpec((1, w), lambda b, c: (0, 0))
    pad16 = lambda v: jnp.pad(v.astype(jnp.float32), (0, LANES - SSM_HEADS)).reshape(1, LANES)
    return pl.pallas_call(
        _ssd_kernel,
        out_shape=jax.ShapeDtypeStruct((t, SSM_WIDTH), jnp.bfloat16),
        grid=(bsz, ns),
        in_specs=[blk(CH_XS), blk(CH_XS + 1), blk(CH_BC), blk(CH_Z), blk(CH_Z + 1),
                  pl.BlockSpec((rows, LANES), lambda b, c: (b * ns + c, 0)),
                  pl.BlockSpec((SSM_CONV, conv_ch), lambda b, c: (0, 0)),
                  row(conv_ch), row(LANES), row(LANES), row(SSM_WIDTH), row(SSM_WIDTH),
                  pl.BlockSpec((2 * LANES, SSM_WIDTH), lambda b, c: (0, 0)),
                  pl.BlockSpec(((SSM_CONV - 1) * L, 2 * L), lambda b, c: (0, 0))],
        out_specs=pl.BlockSpec((rows, SSM_WIDTH), lambda b, c: (b * ns + c, 0)),
        scratch_shapes=[pltpu.VMEM((2, L, conv_ch), jnp.bfloat16),
                        pltpu.VMEM((SSM_STATE, SSM_WIDTH), jnp.float32)],
        compiler_params=_cparams(("arbitrary", "arbitrary")),
        name="ssd_branch",
    )(p, p, p, p, p, dt_raw, conv_w, conv_b.reshape(1, conv_ch), pad16(dt_bias), pad16(a_log),
      jnp.repeat(d_skip.astype(jnp.float32), SSM_HEAD_DIM).reshape(1, SSM_WIDTH),
      ssm_norm_w.reshape(1, SSM_WIDTH), _expand_matrix(), _conv_shift_matrix())


def _merge_kernel(*refs, final):
    if final:
        (oa_ref, ys_ref, ga0_ref, ga1_ref, gs0_ref, gs1_ref, x_ref, gate_ref,
         wa_ref, ws_ref, wo_ref, fw_ref, o_ref) = refs
    else:
        (oa_ref, ys_ref, ga0_ref, ga1_ref, gs0_ref, gs1_ref, x_ref, gate_ref,
         wa_ref, ws_ref, wo_ref, o_ref) = refs
    y_att = jnp.dot(oa_ref[...], wa_ref[...], preferred_element_type=jnp.float32)
    y_ssm = jnp.dot(ys_ref[...], ws_ref[...], preferred_element_type=jnp.float32)
    mg_att = jnp.concatenate([ga0_ref[...], ga1_ref[...]], axis=1).astype(jnp.float32)
    mg_ssm = jnp.concatenate([gs0_ref[...], gs1_ref[...]], axis=1).astype(jnp.float32)
    merged = _sigmoid(mg_att) * y_att + _sigmoid(mg_ssm) * y_ssm
    out = x_ref[...] + gate_ref[...] * jnp.dot(merged.astype(jnp.bfloat16), wo_ref[...],
                                                preferred_element_type=jnp.float32)
    if final:
        out = out * lax.rsqrt(jnp.mean(out * out, axis=-1, keepdims=True) + EPS) * fw_ref[...]
    o_ref[...] = out


def _merge_out(o_att, y_ssm, p, x2, seq, gate, w_att, w_ssm, w_out, final_w):
    t, d = x2.shape
    tm = min(TM_OUT, seq)
    per_b = seq // tm
    final = final_w is not None
    chunk = lambda ch: pl.BlockSpec((None, tm, CHUNK_COLS), lambda m: (ch, m, 0))
    res = lambda shape: pl.BlockSpec(shape, lambda m: (0, 0))
    in_specs = [pl.BlockSpec((tm, ATT_WIDTH), lambda m: (m, 0)),
                pl.BlockSpec((tm, SSM_WIDTH), lambda m: (m, 0)),
                chunk(CH_MGA), chunk(CH_MGA + 1), chunk(CH_MGS), chunk(CH_MGS + 1),
                pl.BlockSpec((tm, d), lambda m: (m, 0)),
                pl.BlockSpec((None, 1, d), lambda m: (m // per_b, 0, 0)),
                res((ATT_WIDTH, d)), res((SSM_WIDTH, d)), res((d, d))]
    args = [o_att, y_ssm, p, p, p, p, x2, gate, w_att, w_ssm, w_out]
    if final:
        in_specs.append(res((1, d)))
        args.append(final_w.reshape(1, d))
    return pl.pallas_call(
        functools.partial(_merge_kernel, final=final),
        out_shape=jax.ShapeDtypeStruct((t, d), jnp.float32),
        grid=(t // tm,),
        in_specs=in_specs,
        out_specs=pl.BlockSpec((tm, d), lambda m: (m, 0)),
        compiler_params=_cparams(("arbitrary",)),
        name="merge_out",
    )(*args)


def _reorder_w_in(w_in_l):
    sizes = [ATT_WIDTH, ATT_WIDTH, ATT_WIDTH, ATT_WIDTH, SSM_WIDTH,
             SSM_WIDTH + 2 * SSM_GROUPS * SSM_STATE, SSM_HEADS, D_MODEL, D_MODEL]
    offs = np.cumsum([0] + sizes)
    seg = lambda i: w_in_l[:, offs[i]:offs[i + 1]]
    q, k, v, g, z, xbc, dtc, mga, mgs = [seg(i) for i in range(9)]
    w_main = jnp.concatenate([q, k, v, g, z, xbc, mga, mgs], axis=1).astype(jnp.bfloat16)
    w_dt = jnp.pad(dtc, ((0, 0), (0, LANES - SSM_HEADS))).astype(jnp.bfloat16)
    return w_main, w_dt


def _expand_matrix():
    e = np.zeros((LANES, SSM_WIDTH), np.float32)
    for j in range(SSM_HEADS):
        e[j, j * SSM_HEAD_DIM:(j + 1) * SSM_HEAD_DIM] = 1.0
    return jnp.asarray(np.concatenate([e, e], axis=0), dtype=jnp.bfloat16)


def _conv_shift_matrix():
    L = SSM_CHUNK
    s = np.zeros(((SSM_CONV - 1) * L, 2 * L), np.float32)
    for tap in range(SSM_CONV - 1):
        for l in range(L):
            s[tap * L + l, L + l - (SSM_CONV - 1 - tap)] = 1.0
    return jnp.asarray(s, dtype=jnp.bfloat16)


def kernel(x, c, positions, w_ada, b_ada, norm_w, w_in, lambda_q1, lambda_k1, lambda_q2, lambda_k2,
           attn_subln_w, w_att_branch, conv_w, conv_b, dt_bias, a_log, d_skip, ssm_norm_w,
           w_ssm_branch, w_out, final_norm_w):
    bsz, seq, d = x.shape
    depth = w_ada.shape[0]
    t = bsz * seq
    x2 = x.reshape(t, d)

    c_pad = jnp.pad(c, ((0, SUBLANES - bsz % SUBLANES), (0, 0))) if bsz % SUBLANES else c
    mod = _ada_mod(c_pad, w_ada, b_ada)[:, :bsz]
    rope_c, rope_sa, rope_sb = _rope_tables(positions)

    for l in range(depth):
        lambda_init = 0.8 - 0.6 * math.exp(-0.3 * l)
        shift = mod[l, :, :d].reshape(bsz, 1, d)
        scale = mod[l, :, d:2 * d].reshape(bsz, 1, d)
        gate = mod[l, :, 2 * d:].reshape(bsz, 1, d)
        w_main, w_dt = _reorder_w_in(w_in[l])
        p, dt_raw = _in_proj(x2, seq, norm_w[l], scale, shift, w_main, w_dt,
                             rope_c, rope_sa, rope_sb)
        lam_prm = jnp.zeros((SUBLANES, LANES), jnp.float32)
        lam_prm = lam_prm.at[0:4, :ATT_HEAD_DIM].set(
            jnp.stack([lambda_q1[l], lambda_k1[l], lambda_q2[l], lambda_k2[l]]))
        o_att = _diff_attention(p, bsz, seq, lam_prm, attn_subln_w[l], lambda_init)
        y_ssm = _ssd_branch(p, dt_raw, bsz, seq, conv_w[l], conv_b[l], dt_bias[l], a_log[l],
                            d_skip[l], ssm_norm_w[l])
        x2 = _merge_out(o_att, y_ssm, p, x2, seq, gate,
                        w_att_branch[l].astype(jnp.bfloat16), w_ssm_branch[l].astype(jnp.bfloat16),
                        w_out[l].astype(jnp.bfloat16),
                        final_norm_w if l == depth - 1 else None)
    return x2.reshape(bsz, seq, d)
```

```python
import functools
import math

import numpy as np
import jax
import jax.numpy as jnp
from jax import lax
from jax.experimental import pallas as pl
from jax.experimental.pallas import tpu as pltpu

D_MODEL = 1024
DEPTH = 2
ATT_HEADS = 4
ATT_HEAD_DIM = 64
ATT_V_DIM = 2 * ATT_HEAD_DIM
ATT_WIDTH = ATT_HEADS * ATT_V_DIM
ROPE_THETA = 500000.0
ROPE_DIM = ATT_HEAD_DIM // 4
SSM_HEAD_DIM = 64
SSM_WIDTH = D_MODEL
SSM_HEADS = SSM_WIDTH // SSM_HEAD_DIM
SSM_GROUPS = 2
SSM_STATE = 128
SSM_CONV = 4
SSM_CHUNK = 128
GROUP_WIDTH = SSM_WIDTH // SSM_GROUPS
HEADS_PER_GROUP = SSM_HEADS // SSM_GROUPS
EPS = 1e-5

LANES = 128
SUBLANES = 8
CHUNK_COLS = 512
CH_Q, CH_K, CH_V, CH_G = 0, 1, 2, 3
CH_Z = 4
CH_XS = 6
CH_BC = 8
CH_MGA = 9
CH_MGS = 11
N_CHUNKS = 13
LOG2_E = math.log2(math.e)
NEG_BIG = -1e30

VMEM_LIMIT = 48 * 1024 * 1024

TM_PROJ = 512
TQ = 1024
ATT_ROWS = 256
TM_OUT = 512
SSD_SUB = 2


def _cparams(sem):
    return pltpu.CompilerParams(dimension_semantics=sem, vmem_limit_bytes=VMEM_LIMIT)


def _sigmoid(x):
    return 0.5 + 0.5 * jnp.tanh(0.5 * x)


def _silu(x):
    h = 0.5 * x
    return h + h * jnp.tanh(h)


def _split3(x):
    hi = x.astype(jnp.bfloat16)
    r1 = x - hi.astype(jnp.float32)
    mid = r1.astype(jnp.bfloat16)
    lo = (r1 - mid.astype(jnp.float32)).astype(jnp.bfloat16)
    return hi, mid, lo


def _ada_kernel(c_ref, w_ref, b_ref, o_ref):
    o_ref[...] = jnp.dot(c_ref[...], w_ref[...], preferred_element_type=jnp.float32,
                         precision=lax.Precision.HIGHEST) + b_ref[...]


def _ada_mod(c_pad, w_ada, b_ada):
    depth, d, d3 = w_ada.shape
    rows = c_pad.shape[0]
    nblk = d3 // d
    return pl.pallas_call(
        _ada_kernel,
        out_shape=jax.ShapeDtypeStruct((depth, rows, d3), jnp.float32),
        grid=(depth, nblk),
        in_specs=[pl.BlockSpec((rows, d), lambda l, n: (0, 0)),
                  pl.BlockSpec((None, d, d), lambda l, n: (l, 0, n)),
                  pl.BlockSpec((None, 1, d), lambda l, n: (l, 0, n))],
        out_specs=pl.BlockSpec((None, rows, d), lambda l, n: (l, 0, n)),
        compiler_params=_cparams(("arbitrary", "arbitrary")),
        name="ada_mod",
    )(c_pad, w_ada, b_ada.reshape(depth, 1, d3))


def _rope_kernel(pos_ref, freq_ref, ma_ref, mb_ref, c_ref, sa_ref, sb_ref):
    ang = pos_ref[...].astype(jnp.float32) * freq_ref[...]
    sn = jnp.sin(ang)
    c_ref[...] = jnp.cos(ang)
    sa_ref[...] = sn * ma_ref[...]
    sb_ref[...] = sn * mb_ref[...]


def _rope_tables(positions):
    t = positions.size
    half = ROPE_DIM // 2
    inv_freq = ROPE_THETA ** (-jnp.arange(0, ROPE_DIM, 2, dtype=jnp.float32) / ROPE_DIM)
    d = np.arange(LANES) % ATT_HEAD_DIM
    in_rope = d < ROPE_DIM
    freq = jnp.where(in_rope, inv_freq[d % half], 0.0).astype(jnp.float32).reshape(1, LANES)
    ma = jnp.asarray((in_rope & (d >= half)).astype(np.float32)).reshape(1, LANES)
    mb = jnp.asarray(-(d < half).astype(np.float32)).reshape(1, LANES)
    pos_b = jnp.broadcast_to(positions.reshape(t, 1), (t, LANES))
    tr = 1024
    row = pl.BlockSpec((1, LANES), lambda i: (0, 0))
    tab = pl.BlockSpec((tr, LANES), lambda i: (i, 0))
    sds = jax.ShapeDtypeStruct((t, LANES), jnp.float32)
    return pl.pallas_call(
        _rope_kernel, out_shape=(sds, sds, sds), grid=(t // tr,),
        in_specs=[tab, row, row, row], out_specs=(tab, tab, tab),
        compiler_params=_cparams(("arbitrary",)), name="rope_tables",
    )(pos_b, freq, ma, mb)


def _inproj_kernel(x_ref, nw_ref, sc_ref, sh_ref, w_ref, wdt_ref, c_ref, sa_ref, sb_ref,
                   p_ref, dt_ref, h_scr):
    x = x_ref[...]
    y = x * lax.rsqrt(jnp.mean(x * x, axis=-1, keepdims=True) + EPS) * nw_ref[...]
    h = y * (1.0 + sc_ref[...]) + sh_ref[...]
    h_scr[...] = h.astype(jnp.bfloat16)
    dt_ref[...] = jnp.dot(h_scr[...], wdt_ref[...], preferred_element_type=jnp.float32)

    def rope(t):
        reps = CHUNK_COLS // LANES
        c = jnp.tile(c_ref[...], (1, reps))
        sa = jnp.tile(sa_ref[...], (1, reps))
        sb = jnp.tile(sb_ref[...], (1, reps))
        half = ROPE_DIM // 2
        return (t * c + pltpu.roll(t, half, 1) * sa + pltpu.roll(t, CHUNK_COLS - half, 1) * sb)

    for n in range(N_CHUNKS):
        acc = jnp.dot(h_scr[...], w_ref[:, n * CHUNK_COLS:(n + 1) * CHUNK_COLS],
                      preferred_element_type=jnp.float32)
        if n == CH_Q:
            acc = rope(acc) * (ATT_HEAD_DIM ** -0.5 * LOG2_E)
        elif n == CH_K:
            acc = rope(acc)
        p_ref[n] = acc.astype(p_ref.dtype)


def _in_proj(x2, seq, norm_w, scale, shift, w_main, w_dt, rope_c, rope_sa, rope_sb):
    t, d = x2.shape
    tm = min(TM_PROJ, seq)
    per_b = seq // tm
    vec = lambda: pl.BlockSpec((None, 1, d), lambda m: (m // per_b, 0, 0))
    tab = lambda: pl.BlockSpec((tm, LANES), lambda m: (m, 0))
    once = lambda shape: pl.BlockSpec(shape, lambda m: (0, 0), pipeline_mode=pl.Buffered(1))
    return pl.pallas_call(
        _inproj_kernel,
        out_shape=(jax.ShapeDtypeStruct((N_CHUNKS, t, CHUNK_COLS), jnp.bfloat16),
                   jax.ShapeDtypeStruct((t, LANES), jnp.float32)),
        grid=(t // tm,),
        in_specs=[pl.BlockSpec((tm, d), lambda m: (m, 0)),
                  once((1, d)), vec(), vec(),
                  once((d, N_CHUNKS * CHUNK_COLS)), once((d, LANES)),
                  tab(), tab(), tab()],
        out_specs=(pl.BlockSpec((N_CHUNKS, tm, CHUNK_COLS), lambda m: (0, m, 0)),
                   pl.BlockSpec((tm, LANES), lambda m: (m, 0))),
        scratch_shapes=[pltpu.VMEM((tm, d), jnp.bfloat16)],
        compiler_params=_cparams(("arbitrary",)),
        name="in_proj",
    )(x2, norm_w.reshape(1, d), scale, shift, w_main, w_dt, rope_c, rope_sa, rope_sb)


def _attn_kernel(lam_ref, q_ref, k_ref, v_ref, g_ref, w_ref, o_ref, qs_scr, vx_scr, sa_scr, sb_scr,
                 m_scr, acc_scr, *, tq, lambda_init):
    qi = pl.program_id(2)

    @pl.when(qi == 0)
    def _():
        vx_scr[:, :LANES] = v_ref[...]
        vx_scr[:, LANES:] = jnp.ones((vx_scr.shape[0], LANES), vx_scr.dtype)

    q = q_ref[...]
    lane = lax.broadcasted_iota(jnp.int32, q.shape, 1)
    zero = jnp.zeros_like(q)
    qs_scr[0:tq, :] = jnp.where(lane < ATT_HEAD_DIM, q, zero)
    qs_scr[tq:, :] = jnp.where(lane >= ATT_HEAD_DIM, q, zero)
    m_scr[...] = jnp.full(m_scr.shape, NEG_BIG, jnp.float32)
    acc_scr[...] = jnp.zeros(acc_scr.shape, jnp.float32)

    tk = tq // 2
    n_chains = 2 * tq // ATT_ROWS

    def visible(rc, diag):
        if diag is None:
            return tk, False
        rel0 = (rc * ATT_ROWS) % tq - diag * tk
        if rel0 >= tk:
            return tk, False
        return max(0, min(tk, rel0 + ATT_ROWS)), True

    def scores(j, dst, diag=None):
        k = k_ref[pl.ds(pl.multiple_of(j * tk, tk), tk), :]
        for rc in range(n_chains):
            if visible(rc, diag)[0] == 0:
                continue
            rows = slice(rc * ATT_ROWS, (rc + 1) * ATT_ROWS)
            dst[rows, :] = lax.dot_general(qs_scr[rows, :], k, (((1,), (1,)), ((), ())),
                                           preferred_element_type=jnp.float32)

    def consume(j, src, diag=None):
        start = pl.multiple_of(j * tk, tk)
        for rc in range(n_chains):
            ncol, masked = visible(rc, diag)
            if ncol == 0:
                continue
            rows = slice(rc * ATT_ROWS, (rc + 1) * ATT_ROWS)
            s = src[rows, :ncol]
            if masked:
                rel0 = (rc * ATT_ROWS) % tq - diag * tk
                r = lax.broadcasted_iota(jnp.int32, s.shape, 0) + rel0
                c = lax.broadcasted_iota(jnp.int32, s.shape, 1)
                s = jnp.where(c <= r, s, NEG_BIG)
            m_prev = m_scr[rows, :]
            m_new = jnp.maximum(m_prev, jnp.max(s, axis=-1, keepdims=True))
            alpha = jnp.exp2(m_prev - m_new)
            p = jnp.exp2(s - jnp.tile(m_new, (1, ncol // LANES)))
            pv = jnp.dot(p.astype(jnp.bfloat16), vx_scr[pl.ds(start, ncol), :],
                         preferred_element_type=jnp.float32)
            acc_scr[rows, :] = jnp.tile(alpha, (1, 2)) * acc_scr[rows, :] + pv
            m_scr[rows, :] = m_new

    scores(0, sa_scr)

    def pair(j):
        scores(j + 1, sb_scr)
        consume(j, sa_scr)
        scores(j + 2, sa_scr)
        consume(j + 1, sb_scr)

    def body(i, carry):
        pair(4 * i)
        pair(4 * i + 2)
        return carry

    n_quads = qi // 2
    lax.fori_loop(0, n_quads, body, 0)

    @pl.when(qi % 2 == 1)
    def _():
        pair(4 * n_quads)

    scores(2 * qi + 1, sb_scr, diag=1)
    consume(2 * qi, sa_scr, diag=0)
    consume(2 * qi + 1, sb_scr, diag=1)

    prm = lam_ref[...]
    lam = (jnp.exp(jnp.sum(prm[0:1] * prm[1:2], axis=-1, keepdims=True))
           - jnp.exp(jnp.sum(prm[2:3] * prm[3:4], axis=-1, keepdims=True)) + lambda_init)
    o_all = acc_scr[:, :LANES] / acc_scr[:, LANES:]
    o = o_all[:tq] - lam * o_all[tq:]
    o = o * lax.rsqrt(jnp.mean(o * o, axis=-1, keepdims=True) + EPS) * w_ref[...]
    o = o * (1.0 - lambda_init)
    o_ref[...] = (o * _silu(g_ref[...].astype(jnp.float32))).astype(o_ref.dtype)


def _diff_attention(p, bsz, seq, lam_prm, subln_w, lambda_init):
    t = bsz * seq
    tq = min(TQ, seq)
    nq = seq // tq
    blk = lambda ch: pl.BlockSpec((None, tq, LANES), lambda b, h, i: (ch, b * nq + i, h))
    full = lambda ch: pl.BlockSpec((None, seq, LANES), lambda b, h, i: (ch, b, h))
    return pl.pallas_call(
        functools.partial(_attn_kernel, tq=tq, lambda_init=lambda_init),
        out_shape=jax.ShapeDtypeStruct((t, ATT_WIDTH), jnp.bfloat16),
        grid=(bsz, ATT_HEADS, nq),
        in_specs=[pl.BlockSpec((SUBLANES, LANES), lambda b, h, i: (0, 0)),
                  blk(CH_Q), full(CH_K), full(CH_V), blk(CH_G),
                  pl.BlockSpec((1, LANES), lambda b, h, i: (0, 0))],
        out_specs=pl.BlockSpec((tq, LANES), lambda b, h, i: (b * nq + i, h)),
        scratch_shapes=[pltpu.VMEM((2 * tq, LANES), jnp.bfloat16),
                        pltpu.VMEM((seq, 2 * LANES), jnp.bfloat16),
                        pltpu.VMEM((2 * tq, tq // 2), jnp.float32),
                        pltpu.VMEM((2 * tq, tq // 2), jnp.float32),
                        pltpu.VMEM((2 * tq, LANES), jnp.float32),
                        pltpu.VMEM((2 * tq, 2 * LANES), jnp.float32)],
        compiler_params=_cparams(("arbitrary", "arbitrary", "arbitrary")),
        name="diff_attn",
    )(lam_prm, p, p, p, p, subln_w.reshape(1, LANES))


def _ssd_kernel(xs0_ref, xs1_ref, bc_ref, z0_ref, z1_ref, dt_ref, cw_ref, cb_ref, dtb_ref,
                alog_ref, dsk_ref, nw_ref, e2_ref, shift_ref, o_ref, carry_scr, h_scr):
    L = SSM_CHUNK
    conv_ch = 3 * CHUNK_COLS
    step = pl.program_id(1)
    parity = step % 2

    @pl.when(step == 0)
    def _():
        carry_scr[1] = jnp.zeros((L, conv_ch), carry_scr.dtype)
        h_scr[...] = jnp.zeros(h_scr.shape, jnp.float32)

    r_i = lax.broadcasted_iota(jnp.int32, (L, L), 0)
    c_i = lax.broadcasted_iota(jnp.int32, (L, L), 1)
    causal = c_i <= r_i
    tril = jnp.where(causal, 1.0, 0.0).astype(jnp.bfloat16)
    tril3 = jnp.concatenate([tril, tril, tril], axis=1)
    lane = lax.broadcasted_iota(jnp.int32, (L, LANES), 1)
    neg_a = -jnp.exp(alog_ref[...])

    prev = carry_scr[1 - parity]
    for sub in range(SSD_SUB):
        rows = slice(sub * L, (sub + 1) * L)

        cur = jnp.concatenate([xs0_ref[rows, :], xs1_ref[rows, :], bc_ref[rows, :]], axis=1)
        shifted = jnp.dot(shift_ref[...], jnp.concatenate([prev, cur], axis=0),
                          preferred_element_type=jnp.float32)
        conv = cb_ref[...] + cur.astype(jnp.float32) * cw_ref[SSM_CONV - 1:SSM_CONV, :]
        for tap in range(SSM_CONV - 1):
            conv = conv + shifted[tap * L:(tap + 1) * L, :] * cw_ref[tap:tap + 1, :]
        prev = cur
        act = _silu(conv)
        xs = act[:, :SSM_WIDTH]
        b_mat = act[:, SSM_WIDTH:SSM_WIDTH + SSM_GROUPS * SSM_STATE]
        c_mat = act[:, SSM_WIDTH + SSM_GROUPS * SSM_STATE:]

        dtr = dt_ref[rows, :] + dtb_ref[...]
        dt = jnp.maximum(dtr, 0.0) + jnp.log1p(jnp.exp(-jnp.abs(dtr)))
        hi, mid, lo = _split3(dt * neg_a)
        acs = jnp.dot(tril3, jnp.concatenate([hi, mid, lo], axis=0),
                      preferred_element_type=jnp.float32) * LOG2_E
        acs_t = acs.T

        both = jnp.concatenate([dt, acs], axis=0)
        hi = both.astype(jnp.bfloat16)
        lo = (both - hi.astype(jnp.float32)).astype(jnp.bfloat16)
        ex = jnp.dot(jnp.concatenate([hi, lo], axis=1), e2_ref[...],
                     preferred_element_type=jnp.float32)
        dt_e = ex[:L]
        acs_e = ex[L:]
        acs_last = acs_e[L - 1:L, :]
        eacs_e = jnp.exp2(acs_e)
        xd = xs * dt_e
        xdd = (xd * jnp.exp2(acs_last - acs_e)).astype(jnp.bfloat16)
        chunk_decay = jnp.exp2(acs_last)

        b_t = b_mat.T
        for g in range(SSM_GROUPS):
            gs = slice(g * GROUP_WIDTH, (g + 1) * GROUP_WIDTH)
            c_g = c_mat[:, g * SSM_STATE:(g + 1) * SSM_STATE].astype(jnp.bfloat16)
            bt_g = b_t[g * SSM_STATE:(g + 1) * SSM_STATE, :].astype(jnp.bfloat16)
            cb = jnp.dot(c_g, bt_g, preferred_element_type=jnp.float32)

            h_g = h_scr[:, gs]
            y_off = jnp.dot(c_g, h_g.astype(jnp.bfloat16),
                            preferred_element_type=jnp.float32) * eacs_e[:, gs]
            st = jnp.dot(bt_g, xdd[:, gs], preferred_element_type=jnp.float32)
            h_scr[:, gs] = h_g * chunk_decay[:, gs] + st

            parts = []
            for pr in range(HEADS_PER_GROUP // 2):
                col0 = g * GROUP_WIDTH + pr * LANES
                xd_pair = xd[:, col0:col0 + LANES]
                ms = []
                for hh in range(2):
                    j = g * HEADS_PER_GROUP + 2 * pr + hh
                    seg = acs[:, j:j + 1] - acs_t[j:j + 1, :]
                    lm = jnp.exp2(jnp.where(causal, seg, NEG_BIG))
                    ms.append((cb * lm).astype(jnp.bfloat16))
                w_pair = jnp.concatenate(
                    [jnp.where(lane < SSM_HEAD_DIM, xd_pair, 0.0),
                     jnp.where(lane >= SSM_HEAD_DIM, xd_pair, 0.0)], axis=0).astype(jnp.bfloat16)
                parts.append(jnp.dot(jnp.concatenate(ms, axis=1), w_pair,
                                     preferred_element_type=jnp.float32))
            y_diag = jnp.concatenate(parts, axis=1)

            y = y_diag + y_off + dsk_ref[:, gs] * xs[:, gs]
            zg = (z0_ref if g == 0 else z1_ref)[rows, :].astype(jnp.float32)
            y = y * _silu(zg)
            y = y * lax.rsqrt(jnp.mean(y * y, axis=-1, keepdims=True) + EPS) * nw_ref[:, gs]
            o_ref[rows, gs] = y.astype(o_ref.dtype)
    carry_scr[parity] = prev


def _ssd_branch(p, dt_raw, bsz, seq, conv_w, conv_b, dt_bias, a_log, d_skip, ssm_norm_w):
    t = bsz * seq
    L = SSM_CHUNK
    rows = SSD_SUB * L
    ns = seq // rows
    conv_ch = 3 * CHUNK_COLS
    blk = lambda ch: pl.BlockSpec((None, rows, CHUNK_COLS), lambda b, c: (ch, b * ns + c, 0))
    row = lambda w: pl.BlockSpec((1, w), lambda b, c: (0, 0))
    pad16 = lambda v: jnp.pad(v.astype(jnp.float32), (0, LANES - SSM_HEADS)).reshape(1, LANES)
    return pl.pallas_call(
        _ssd_kernel,
        out_shape=jax.ShapeDtypeStruct((t, SSM_WIDTH), jnp.bfloat16),
        grid=(bsz, ns),
        in_specs=[blk(CH_XS), blk(CH_XS + 1), blk(CH_BC), blk(CH_Z), blk(CH_Z + 1),
                  pl.BlockSpec((rows, LANES), lambda b, c: (b * ns + c, 0)),
                  pl.BlockSpec((SSM_CONV, conv_ch), lambda b, c: (0, 0)),
                  row(conv_ch), row(LANES), row(LANES), row(SSM_WIDTH), row(SSM_WIDTH),
                  pl.BlockSpec((2 * LANES, SSM_WIDTH), lambda b, c: (0, 0)),
                  pl.BlockSpec(((SSM_CONV - 1) * L, 2 * L), lambda b, c: (0, 0))],
        out_specs=pl.BlockSpec((rows, SSM_WIDTH), lambda b, c: (b * ns + c, 0)),
        scratch_shapes=[pltpu.VMEM((2, L, conv_ch), jnp.bfloat16),
                        pltpu.VMEM((SSM_STATE, SSM_WIDTH), jnp.float32)],
        compiler_params=_cparams(("arbitrary", "arbitrary")),
        name="ssd_branch",
    )(p, p, p, p, p, dt_raw, conv_w, conv_b.reshape(1, conv_ch), pad16(dt_bias), pad16(a_log),
      jnp.repeat(d_skip.astype(jnp.float32), SSM_HEAD_DIM).reshape(1, SSM_WIDTH),
      ssm_norm_w.reshape(1, SSM_WIDTH), _expand_matrix(), _conv_shift_matrix())


def _merge_kernel(*refs, final):
    if final:
        (oa_ref, ys_ref, ga0_ref, ga1_ref, gs0_ref, gs1_ref, x_ref, gate_ref,
         wa_ref, ws_ref, wo_ref, fw_ref, o_ref) = refs
    else:
        (oa_ref, ys_ref, ga0_ref, ga1_ref, gs0_ref, gs1_ref, x_ref, gate_ref,
         wa_ref, ws_ref, wo_ref, o_ref) = refs
    y_att = jnp.dot(oa_ref[...], wa_ref[...], preferred_element_type=jnp.float32)
    y_ssm = jnp.dot(ys_ref[...], ws_ref[...], preferred_element_type=jnp.float32)
    mg_att = jnp.concatenate([ga0_ref[...], ga1_ref[...]], axis=1).astype(jnp.float32)
    mg_ssm = jnp.concatenate([gs0_ref[...], gs1_ref[...]], axis=1).astype(jnp.float32)
    merged = _sigmoid(mg_att) * y_att + _sigmoid(mg_ssm) * y_ssm
    out = x_ref[...] + gate_ref[...] * jnp.dot(merged.astype(jnp.bfloat16), wo_ref[...],
                                                preferred_element_type=jnp.float32)
    if final:
        out = out * lax.rsqrt(jnp.mean(out * out, axis=-1, keepdims=True) + EPS) * fw_ref[...]
    o_ref[...] = out


def _merge_out(o_att, y_ssm, p, x2, seq, gate, w_att, w_ssm, w_out, final_w):
    t, d = x2.shape
    tm = min(TM_OUT, seq)
    per_b = seq // tm
    final = final_w is not None
    chunk = lambda ch: pl.BlockSpec((None, tm, CHUNK_COLS), lambda m: (ch, m, 0))
    res = lambda shape: pl.BlockSpec(shape, lambda m: (0, 0))
    in_specs = [pl.BlockSpec((tm, ATT_WIDTH), lambda m: (m, 0)),
                pl.BlockSpec((tm, SSM_WIDTH), lambda m: (m, 0)),
                chunk(CH_MGA), chunk(CH_MGA + 1), chunk(CH_MGS), chunk(CH_MGS + 1),
                pl.BlockSpec((tm, d), lambda m: (m, 0)),
                pl.BlockSpec((None, 1, d), lambda m: (m // per_b, 0, 0)),
                res((ATT_WIDTH, d)), res((SSM_WIDTH, d)), res((d, d))]
    args = [o_att, y_ssm, p, p, p, p, x2, gate, w_att, w_ssm, w_out]
    if final:
        in_specs.append(res((1, d)))
        args.append(final_w.reshape(1, d))
    return pl.pallas_call(
        functools.partial(_merge_kernel, final=final),
        out_shape=jax.ShapeDtypeStruct((t, d), jnp.float32),
        grid=(t // tm,),
        in_specs=in_specs,
        out_specs=pl.BlockSpec((tm, d), lambda m: (m, 0)),
        compiler_params=_cparams(("arbitrary",)),
        name="merge_out",
    )(*args)


def _reorder_w_in(w_in_l):
    sizes = [ATT_WIDTH, ATT_WIDTH, ATT_WIDTH, ATT_WIDTH, SSM_WIDTH,
             SSM_WIDTH + 2 * SSM_GROUPS * SSM_STATE, SSM_HEADS, D_MODEL, D_MODEL]
    offs = np.cumsum([0] + sizes)
    seg = lambda i: w_in_l[:, offs[i]:offs[i + 1]]
    q, k, v, g, z, xbc, dtc, mga, mgs = [seg(i) for i in range(9)]
    w_main = jnp.concatenate([q, k, v, g, z, xbc, mga, mgs], axis=1).astype(jnp.bfloat16)
    w_dt = jnp.pad(dtc, ((0, 0), (0, LANES - SSM_HEADS))).astype(jnp.bfloat16)
    return w_main, w_dt


def _expand_matrix():
    e = np.zeros((LANES, SSM_WIDTH), np.float32)
    for j in range(SSM_HEADS):
        e[j, j * SSM_HEAD_DIM:(j + 1) * SSM_HEAD_DIM] = 1.0
    return jnp.asarray(np.concatenate([e, e], axis=0), dtype=jnp.bfloat16)


def _conv_shift_matrix():
    L = SSM_CHUNK
    s = np.zeros(((SSM_CONV - 1) * L, 2 * L), np.float32)
    for tap in range(SSM_CONV - 1):
        for l in range(L):
            s[tap * L + l, L + l - (SSM_CONV - 1 - tap)] = 1.0
    return jnp.asarray(s, dtype=jnp.bfloat16)


def kernel(x, c, positions, w_ada, b_ada, norm_w, w_in, lambda_q1, lambda_k1, lambda_q2, lambda_k2,
           attn_subln_w, w_att_branch, conv_w, conv_b, dt_bias, a_log, d_skip, ssm_norm_w,
           w_ssm_branch, w_out, final_norm_w):
    bsz, seq, d = x.shape
    depth = w_ada.shape[0]
    t = bsz * seq
    x2 = x.reshape(t, d)

    c_pad = jnp.pad(c, ((0, SUBLANES - bsz % SUBLANES), (0, 0))) if bsz % SUBLANES else c
    mod = _ada_mod(c_pad, w_ada, b_ada)[:, :bsz]
    rope_c, rope_sa, rope_sb = _rope_tables(positions)

    for l in range(depth):
        lambda_init = 0.8 - 0.6 * math.exp(-0.3 * l)
        shift = mod[l, :, :d].reshape(bsz, 1, d)
        scale = mod[l, :, d:2 * d].reshape(bsz, 1, d)
        gate = mod[l, :, 2 * d:].reshape(bsz, 1, d)
        w_main, w_dt = _reorder_w_in(w_in[l])
        p, dt_raw = _in_proj(x2, seq, norm_w[l], scale, shift, w_main, w_dt,
                             rope_c, rope_sa, rope_sb)
        lam_prm = jnp.zeros((SUBLANES, LANES), jnp.float32)
        lam_prm = lam_prm.at[0:4, :ATT_HEAD_DIM].set(
            jnp.stack([lambda_q1[l], lambda_k1[l], lambda_q2[l], lambda_k2[l]]))
        o_att = _diff_attention(p, bsz, seq, lam_prm, attn_subln_w[l], lambda_init)
        y_ssm = _ssd_branch(p, dt_raw, bsz, seq, conv_w[l], conv_b[l], dt_bias[l], a_log[l],
                            d_skip[l], ssm_norm_w[l])
        x2 = _merge_out(o_att, y_ssm, p, x2, seq, gate,
                        w_att_branch[l].astype(jnp.bfloat16), w_ssm_branch[l].astype(jnp.bfloat16),
                        w_out[l].astype(jnp.bfloat16),
                        final_norm_w if l == depth - 1 else None)
    return x2.reshape(bsz, seq, d)
```

```python
import functools
import math

import numpy as np
import jax
import jax.numpy as jnp
from jax import lax
from jax.experimental import pallas as pl
from jax.experimental.pallas import tpu as pltpu

D_MODEL = 1024
ATT_HEADS = 4
ATT_HEAD_DIM = 64
ATT_V_DIM = 2 * ATT_HEAD_DIM
ATT_WIDTH = ATT_HEADS * ATT_V_DIM
ROPE_THETA = 500000.0
ROPE_DIM = ATT_HEAD_DIM // 4
SSM_HEAD_DIM = 64
SSM_WIDTH = D_MODEL
SSM_HEADS = SSM_WIDTH // SSM_HEAD_DIM
SSM_GROUPS = 2
SSM_STATE = 128
SSM_CONV = 4
SSM_CHUNK = 128
GROUP_WIDTH = SSM_WIDTH // SSM_GROUPS
HEADS_PER_GROUP = SSM_HEADS // SSM_GROUPS
EPS = 1e-5

LANES = 128
SUBLANES = 8
CHUNK_COLS = 512
CH_Q, CH_K, CH_V, CH_G = 0, 1, 2, 3
CH_Z = 4
CH_XS = 6
CH_BC = 8
CH_MGA = 9
CH_MGS = 11
N_CHUNKS = 13
LOG2_E = math.log2(math.e)
NEG_BIG = -1e30

V7X_VMEM_BYTES = 64 * 1024 * 1024
VMEM_LIMIT = V7X_VMEM_BYTES * 3 // 4

TM_PROJ = 512
TQ = 1024
TK = 512
ATT_ROWS = 256
TM_OUT = 512
SSD_SUB = 4


def _cparams(sem):
    return pltpu.CompilerParams(dimension_semantics=sem, vmem_limit_bytes=VMEM_LIMIT)


def _sigmoid(x):
    return 0.5 + 0.5 * jnp.tanh(0.5 * x)


def _silu(x):
    h = 0.5 * x
    return h + h * jnp.tanh(h)


def _split3(x):
    hi = x.astype(jnp.bfloat16)
    r1 = x - hi.astype(jnp.float32)
    mid = r1.astype(jnp.bfloat16)
    lo = (r1 - mid.astype(jnp.float32)).astype(jnp.bfloat16)
    return hi, mid, lo


def _ada_kernel(c_ref, w_ref, b_ref, o_ref):
    o_ref[...] = jnp.dot(c_ref[...], w_ref[...], preferred_element_type=jnp.float32,
                         precision=lax.Precision.HIGHEST) + b_ref[...]


def _ada_mod(c_pad, w_ada, b_ada):
    depth, d, d3 = w_ada.shape
    rows = c_pad.shape[0]
    nblk = d3 // d
    return pl.pallas_call(
        _ada_kernel,
        out_shape=jax.ShapeDtypeStruct((depth, rows, d3), jnp.float32),
        grid=(depth, nblk),
        in_specs=[pl.BlockSpec((rows, d), lambda l, n: (0, 0)),
                  pl.BlockSpec((None, d, d), lambda l, n: (l, 0, n)),
                  pl.BlockSpec((None, 1, d), lambda l, n: (l, 0, n))],
        out_specs=pl.BlockSpec((None, rows, d), lambda l, n: (l, 0, n)),
        compiler_params=_cparams(("arbitrary", "arbitrary")),
        name="ada_mod",
    )(c_pad, w_ada, b_ada.reshape(depth, 1, d3))


def _rope_kernel(pos_ref, freq_ref, ma_ref, mb_ref, c_ref, sa_ref, sb_ref):
    ang = pos_ref[...].astype(jnp.float32) * freq_ref[...]
    sn = jnp.sin(ang)
    c_ref[...] = jnp.cos(ang)
    sa_ref[...] = sn * ma_ref[...]
    sb_ref[...] = sn * mb_ref[...]


def _rope_tables(positions):
    t = positions.size
    half = ROPE_DIM // 2
    inv_freq = ROPE_THETA ** (-jnp.arange(0, ROPE_DIM, 2, dtype=jnp.float32) / ROPE_DIM)
    d = np.arange(LANES) % ATT_HEAD_DIM
    in_rope = d < ROPE_DIM
    freq = jnp.where(in_rope, inv_freq[d % half], 0.0).astype(jnp.float32).reshape(1, LANES)
    ma = jnp.asarray((in_rope & (d >= half)).astype(np.float32)).reshape(1, LANES)
    mb = jnp.asarray(-(d < half).astype(np.float32)).reshape(1, LANES)
    pos_b = jnp.broadcast_to(positions.reshape(t, 1), (t, LANES))
    tr = 1024
    row = pl.BlockSpec((1, LANES), lambda i: (0, 0))
    tab = pl.BlockSpec((tr, LANES), lambda i: (i, 0))
    sds = jax.ShapeDtypeStruct((t, LANES), jnp.float32)
    return pl.pallas_call(
        _rope_kernel, out_shape=(sds, sds, sds), grid=(t // tr,),
        in_specs=[tab, row, row, row], out_specs=(tab, tab, tab),
        compiler_params=_cparams(("arbitrary",)), name="rope_tables",
    )(pos_b, freq, ma, mb)


def _inproj_kernel(x_ref, nw_ref, sc_ref, sh_ref, w_ref, wdt_ref, c_ref, sa_ref, sb_ref,
                   p_ref, dt_ref, h_scr):
    x = x_ref[...]
    y = x * lax.rsqrt(jnp.mean(x * x, axis=-1, keepdims=True) + EPS) * nw_ref[...]
    h = y * (1.0 + sc_ref[...]) + sh_ref[...]
    h_scr[...] = h.astype(jnp.bfloat16)
    dt_ref[...] = jnp.dot(h_scr[...], wdt_ref[...], preferred_element_type=jnp.float32)

    def rope(t):
        reps = CHUNK_COLS // LANES
        c = jnp.tile(c_ref[...], (1, reps))
        sa = jnp.tile(sa_ref[...], (1, reps))
        sb = jnp.tile(sb_ref[...], (1, reps))
        half = ROPE_DIM // 2
        return (t * c + pltpu.roll(t, half, 1) * sa + pltpu.roll(t, CHUNK_COLS - half, 1) * sb)

    for n in range(N_CHUNKS):
        acc = jnp.dot(h_scr[...], w_ref[:, n * CHUNK_COLS:(n + 1) * CHUNK_COLS],
                      preferred_element_type=jnp.float32)
        if n == CH_Q:
            acc = rope(acc) * (ATT_HEAD_DIM ** -0.5 * LOG2_E)
        elif n == CH_K:
            acc = rope(acc)
        p_ref[n] = acc.astype(p_ref.dtype)


def _in_proj(x2, seq, norm_w, scale, shift, w_main, w_dt, rope_c, rope_sa, rope_sb):
    t, d = x2.shape
    tm = min(TM_PROJ, seq)
    per_b = seq // tm
    vec = lambda: pl.BlockSpec((None, 1, d), lambda m: (m // per_b, 0, 0))
    tab = lambda: pl.BlockSpec((tm, LANES), lambda m: (m, 0))
    once = lambda shape: pl.BlockSpec(shape, lambda m: (0, 0), pipeline_mode=pl.Buffered(1))
    return pl.pallas_call(
        _inproj_kernel,
        out_shape=(jax.ShapeDtypeStruct((N_CHUNKS, t, CHUNK_COLS), jnp.bfloat16),
                   jax.ShapeDtypeStruct((t, LANES), jnp.float32)),
        grid=(t // tm,),
        in_specs=[pl.BlockSpec((tm, d), lambda m: (m, 0)),
                  once((1, d)), vec(), vec(),
                  once((d, N_CHUNKS * CHUNK_COLS)), once((d, LANES)),
                  tab(), tab(), tab()],
        out_specs=(pl.BlockSpec((N_CHUNKS, tm, CHUNK_COLS), lambda m: (0, m, 0)),
                   pl.BlockSpec((tm, LANES), lambda m: (m, 0))),
        scratch_shapes=[pltpu.VMEM((tm, d), jnp.bfloat16)],
        compiler_params=_cparams(("arbitrary",)),
        name="in_proj",
    )(x2, norm_w.reshape(1, d), scale, shift, w_main, w_dt, rope_c, rope_sa, rope_sb)


def _attn_kernel(lam_ref, q_ref, k_ref, v_ref, g_ref, w_ref, o_ref, qs_scr, vx_scr, sa_scr, sb_scr,
                 m_scr, acc_scr, *, tq, lambda_init):
    qi = pl.program_id(2)

    @pl.when(qi == 0)
    def _():
        vx_scr[:, :LANES] = v_ref[...]
        vx_scr[:, LANES:] = jnp.ones((vx_scr.shape[0], LANES), vx_scr.dtype)

    q = q_ref[...]
    lane = lax.broadcasted_iota(jnp.int32, q.shape, 1)
    zero = jnp.zeros_like(q)
    qs_scr[0:tq, :] = jnp.where(lane < ATT_HEAD_DIM, q, zero)
    qs_scr[tq:, :] = jnp.where(lane >= ATT_HEAD_DIM, q, zero)
    m_scr[...] = jnp.full(m_scr.shape, NEG_BIG, jnp.float32)
    acc_scr[...] = jnp.zeros(acc_scr.shape, jnp.float32)

    tk = min(TK, tq // 2)
    n_diag = tq // tk
    assert tq % (2 * tk) == 0
    n_chains = 2 * tq // ATT_ROWS

    def visible(rc, diag):
        if diag is None:
            return tk, False
        rel0 = (rc * ATT_ROWS) % tq - diag * tk
        if rel0 >= tk:
            return tk, False
        return max(0, min(tk, rel0 + ATT_ROWS)), True

    def scores(j, dst, diag=None):
        k = k_ref[pl.ds(pl.multiple_of(j * tk, tk), tk), :]
        for rc in range(n_chains):
            if visible(rc, diag)[0] == 0:
                continue
            rows = slice(rc * ATT_ROWS, (rc + 1) * ATT_ROWS)
            dst[rows, :] = lax.dot_general(qs_scr[rows, :], k, (((1,), (1,)), ((), ())),
                                           preferred_element_type=jnp.float32)

    def consume(j, src, diag=None):
        start = pl.multiple_of(j * tk, tk)
        for rc in range(n_chains):
            ncol, masked = visible(rc, diag)
            if ncol == 0:
                continue
            rows = slice(rc * ATT_ROWS, (rc + 1) * ATT_ROWS)
            s = src[rows, :ncol]
            if masked:
                rel0 = (rc * ATT_ROWS) % tq - diag * tk
                r = lax.broadcasted_iota(jnp.int32, s.shape, 0) + rel0
                c = lax.broadcasted_iota(jnp.int32, s.shape, 1)
                s = jnp.where(c <= r, s, NEG_BIG)
            m_prev = m_scr[rows, :]
            m_new = jnp.maximum(m_prev, jnp.max(s, axis=-1, keepdims=True))
            alpha = jnp.exp2(m_prev - m_new)
            p = jnp.exp2(s - jnp.tile(m_new, (1, ncol // LANES)))
            pv = jnp.dot(p.astype(jnp.bfloat16), vx_scr[pl.ds(start, ncol), :],
                         preferred_element_type=jnp.float32)
            acc_scr[rows, :] = jnp.tile(alpha, (1, 2)) * acc_scr[rows, :] + pv
            m_scr[rows, :] = m_new

    scores(0, sa_scr)

    def pair(j):
        scores(j + 1, sb_scr)
        consume(j, sa_scr)
        scores(j + 2, sa_scr)
        consume(j + 1, sb_scr)

    def body(i, carry):
        pair(4 * i)
        pair(4 * i + 2)
        return carry

    n_full = n_diag * qi
    n_quads = n_full // 4
    lax.fori_loop(0, n_quads, body, 0)

    @pl.when(n_full % 4 == 2)
    def _():
        pair(4 * n_quads)

    bufs = (sa_scr, sb_scr)
    for d in range(n_diag):
        if d + 1 < n_diag:
            scores(n_full + d + 1, bufs[(d + 1) % 2], diag=d + 1)
        consume(n_full + d, bufs[d % 2], diag=d)

    prm = lam_ref[...]
    lam = (jnp.exp(jnp.sum(prm[0:1] * prm[1:2], axis=-1, keepdims=True))
           - jnp.exp(jnp.sum(prm[2:3] * prm[3:4], axis=-1, keepdims=True)) + lambda_init)
    o_all = acc_scr[:, :LANES] / acc_scr[:, LANES:]
    o = o_all[:tq] - lam * o_all[tq:]
    o = o * lax.rsqrt(jnp.mean(o * o, axis=-1, keepdims=True) + EPS) * w_ref[...]
    o = o * (1.0 - lambda_init)
    o_ref[...] = (o * _silu(g_ref[...].astype(jnp.float32))).astype(o_ref.dtype)


def _diff_attention(p, bsz, seq, lam_prm, subln_w, lambda_init):
    t = bsz * seq
    tq = min(TQ, seq)
    tk = min(TK, tq // 2)
    nq = seq // tq
    blk = lambda ch: pl.BlockSpec((None, tq, LANES), lambda b, h, i: (ch, b * nq + i, h))
    full = lambda ch: pl.BlockSpec((None, seq, LANES), lambda b, h, i: (ch, b, h))
    return pl.pallas_call(
        functools.partial(_attn_kernel, tq=tq, lambda_init=lambda_init),
        out_shape=jax.ShapeDtypeStruct((t, ATT_WIDTH), jnp.bfloat16),
        grid=(bsz, ATT_HEADS, nq),
        in_specs=[pl.BlockSpec((SUBLANES, LANES), lambda b, h, i: (0, 0)),
                  blk(CH_Q), full(CH_K), full(CH_V), blk(CH_G),
                  pl.BlockSpec((1, LANES), lambda b, h, i: (0, 0))],
        out_specs=pl.BlockSpec((tq, LANES), lambda b, h, i: (b * nq + i, h)),
        scratch_shapes=[pltpu.VMEM((2 * tq, LANES), jnp.bfloat16),
                        pltpu.VMEM((seq, 2 * LANES), jnp.bfloat16),
                        pltpu.VMEM((2 * tq, tk), jnp.float32),
                        pltpu.VMEM((2 * tq, tk), jnp.float32),
                        pltpu.VMEM((2 * tq, LANES), jnp.float32),
                        pltpu.VMEM((2 * tq, 2 * LANES), jnp.float32)],
        compiler_params=_cparams(("arbitrary", "arbitrary", "arbitrary")),
        name="diff_attn",
    )(lam_prm, p, p, p, p, subln_w.reshape(1, LANES))


def _ssd_kernel(xs0_ref, xs1_ref, bc_ref, z0_ref, z1_ref, dt_ref, cw_ref, cb_ref, dtb_ref,
                alog_ref, dsk_ref, nw_ref, e2_ref, shift_ref, o_ref, carry_scr, h_scr):
    L = SSM_CHUNK
    conv_ch = 3 * CHUNK_COLS
    step = pl.program_id(1)
    parity = step % 2

    @pl.when(step == 0)
    def _():
        carry_scr[1] = jnp.zeros((L, conv_ch), carry_scr.dtype)
        h_scr[...] = jnp.zeros(h_scr.shape, jnp.float32)

    r_i = lax.broadcasted_iota(jnp.int32, (L, L), 0)
    c_i = lax.broadcasted_iota(jnp.int32, (L, L), 1)
    causal = c_i <= r_i
    tril = jnp.where(causal, 1.0, 0.0).astype(jnp.bfloat16)
    tril3 = jnp.concatenate([tril, tril, tril], axis=1)
    lane = lax.broadcasted_iota(jnp.int32, (L, LANES), 1)
    neg_a = -jnp.exp(alog_ref[...])

    prev = carry_scr[1 - parity]
    for sub in range(SSD_SUB):
        rows = slice(sub * L, (sub + 1) * L)

        cur = jnp.concatenate([xs0_ref[rows, :], xs1_ref[rows, :], bc_ref[rows, :]], axis=1)
        shifted = jnp.dot(shift_ref[...], jnp.concatenate([prev, cur], axis=0),
                          preferred_element_type=jnp.float32)
        conv = cb_ref[...] + cur.astype(jnp.float32) * cw_ref[SSM_CONV - 1:SSM_CONV, :]
        for tap in range(SSM_CONV - 1):
            conv = conv + shifted[tap * L:(tap + 1) * L, :] * cw_ref[tap:tap + 1, :]
        prev = cur
        act = _silu(conv)
        xs = act[:, :SSM_WIDTH]
        b_mat = act[:, SSM_WIDTH:SSM_WIDTH + SSM_GROUPS * SSM_STATE]
        c_mat = act[:, SSM_WIDTH + SSM_GROUPS * SSM_STATE:]

        dtr = dt_ref[rows, :] + dtb_ref[...]
        dt = jnp.maximum(dtr, 0.0) + jnp.log1p(jnp.exp(-jnp.abs(dtr)))
        hi, mid, lo = _split3(dt * neg_a)
        acs = jnp.dot(tril3, jnp.concatenate([hi, mid, lo], axis=0),
                      preferred_element_type=jnp.float32) * LOG2_E
        acs_t = acs.T

        both = jnp.concatenate([dt, acs], axis=0)
        hi = both.astype(jnp.bfloat16)
        lo = (both - hi.astype(jnp.float32)).astype(jnp.bfloat16)
        ex = jnp.dot(jnp.concatenate([hi, lo], axis=1), e2_ref[...],
                     preferred_element_type=jnp.float32)
        dt_e = ex[:L]
        acs_e = ex[L:]
        acs_last = acs_e[L - 1:L, :]
        eacs_e = jnp.exp2(acs_e)
        xd = xs * dt_e
        xdd = (xd * jnp.exp2(acs_last - acs_e)).astype(jnp.bfloat16)
        chunk_decay = jnp.exp2(acs_last)

        b_t = b_mat.T
        for g in range(SSM_GROUPS):
            gs = slice(g * GROUP_WIDTH, (g + 1) * GROUP_WIDTH)
            c_g = c_mat[:, g * SSM_STATE:(g + 1) * SSM_STATE].astype(jnp.bfloat16)
            bt_g = b_t[g * SSM_STATE:(g + 1) * SSM_STATE, :].astype(jnp.bfloat16)
            cb = jnp.dot(c_g, bt_g, preferred_element_type=jnp.float32)

            h_g = h_scr[:, gs]
            y_off = jnp.dot(c_g, h_g.astype(jnp.bfloat16),
                            preferred_element_type=jnp.float32) * eacs_e[:, gs]
            st = jnp.dot(bt_g, xdd[:, gs], preferred_element_type=jnp.float32)
            h_scr[:, gs] = h_g * chunk_decay[:, gs] + st

            parts = []
            for pr in range(HEADS_PER_GROUP // 2):
                col0 = g * GROUP_WIDTH + pr * LANES
                xd_pair = xd[:, col0:col0 + LANES]
                ms = []
                for hh in range(2):
                    j = g * HEADS_PER_GROUP + 2 * pr + hh
                    seg = acs[:, j:j + 1] - acs_t[j:j + 1, :]
                    lm = jnp.exp2(jnp.where(causal, seg, NEG_BIG))
                    ms.append((cb * lm).astype(jnp.bfloat16))
                w_pair = jnp.concatenate(
                    [jnp.where(lane < SSM_HEAD_DIM, xd_pair, 0.0),
                     jnp.where(lane >= SSM_HEAD_DIM, xd_pair, 0.0)], axis=0).astype(jnp.bfloat16)
                parts.append(jnp.dot(jnp.concatenate(ms, axis=1), w_pair,
                                     preferred_element_type=jnp.float32))
            y_diag = jnp.concatenate(parts, axis=1)

            y = y_diag + y_off + dsk_ref[:, gs] * xs[:, gs]
            zg = (z0_ref if g == 0 else z1_ref)[rows, :].astype(jnp.float32)
            y = y * _silu(zg)
            y = y * lax.rsqrt(jnp.mean(y * y, axis=-1, keepdims=True) + EPS) * nw_ref[:, gs]
            o_ref[rows, gs] = y.astype(o_ref.dtype)
    carry_scr[parity] = prev


def _ssd_branch(p, dt_raw, bsz, seq, conv_w, conv_b, dt_bias, a_log, d_skip, ssm_norm_w):
    t = bsz * seq
    L = SSM_CHUNK
    rows = SSD_SUB * L
    ns = seq // rows
    conv_ch = 3 * CHUNK_COLS
    blk = lambda ch: pl.BlockSpec((None, rows, CHUNK_COLS), lambda b, c: (ch, b * ns + c, 0))
    row = lambda w: pl.BlockSpec((1, w), lambda b, c: (0, 0))
    pad16 = lambda v: jnp.pad(v.astype(jnp.float32), (0, LANES - SSM_HEADS)).reshape(1, LANES)
    return pl.pallas_call(
        _ssd_kernel,
        out_shape=jax.ShapeDtypeStruct((t, SSM_WIDTH), jnp.bfloat16),
        grid=(bsz, ns),
        in_specs=[blk(CH_XS), blk(CH_XS + 1), blk(CH_BC), blk(CH_Z), blk(CH_Z + 1),
                  pl.BlockSpec((rows, LANES), lambda b, c: (b * ns + c, 0)),
                  pl.BlockSpec((SSM_CONV, conv_ch), lambda b, c: (0, 0)),
                  row(conv_ch), row(LANES), row(LANES), row(SSM_WIDTH), row(SSM_WIDTH),
                  pl.BlockSpec((2 * LANES, SSM_WIDTH), lambda b, c: (0, 0)),
                  pl.BlockSpec(((SSM_CONV - 1) * L, 2 * L), lambda b, c: (0, 0))],
        out_specs=pl.BlockSpec((rows, SSM_WIDTH), lambda b, c: (b * ns + c, 0)),
        scratch_shapes=[pltpu.VMEM((2, L, conv_ch), jnp.bfloat16),
                        pltpu.VMEM((SSM_STATE, SSM_WIDTH), jnp.float32)],
        compiler_params=_cparams(("arbitrary", "arbitrary")),
        name="ssd_branch",
    )(p, p, p, p, p, dt_raw, conv_w, conv_b.reshape(1, conv_ch), pad16(dt_bias), pad16(a_log),
      jnp.repeat(d_skip.astype(jnp.float32), SSM_HEAD_DIM).reshape(1, SSM_WIDTH),
      ssm_norm_w.reshape(1, SSM_WIDTH), _expand_matrix(), _conv_shift_matrix())


def _merge_kernel(*refs, final):
    if final:
        (oa_ref, ys_ref, ga0_ref, ga1_ref, gs0_ref, gs1_ref, x_ref, gate_ref,
         wa_ref, ws_ref, wo_ref, fw_ref, o_ref) = refs
    else:
        (oa_ref, ys_ref, ga0_ref, ga1_ref, gs0_ref, gs1_ref, x_ref, gate_ref,
         wa_ref, ws_ref, wo_ref, o_ref) = refs
    y_att = jnp.dot(oa_ref[...], wa_ref[...], preferred_element_type=jnp.float32)
    y_ssm = jnp.dot(ys_ref[...], ws_ref[...], preferred_element_type=jnp.float32)
    mg_att = jnp.concatenate([ga0_ref[...], ga1_ref[...]], axis=1).astype(jnp.float32)
    mg_ssm = jnp.concatenate([gs0_ref[...], gs1_ref[...]], axis=1).astype(jnp.float32)
    merged = _sigmoid(mg_att) * y_att + _sigmoid(mg_ssm) * y_ssm
    out = x_ref[...] + gate_ref[...] * jnp.dot(merged.astype(jnp.bfloat16), wo_ref[...],
                                                preferred_element_type=jnp.float32)
    if final:
        out = out * lax.rsqrt(jnp.mean(out * out, axis=-1, keepdims=True) + EPS) * fw_ref[...]
    o_ref[...] = out


def _merge_out(o_att, y_ssm, p, x2, seq, gate, w_att, w_ssm, w_out, final_w):
    t, d = x2.shape
    tm = min(TM_OUT, seq)
    per_b = seq // tm
    final = final_w is not None
    chunk = lambda ch: pl.BlockSpec((None, tm, CHUNK_COLS), lambda m: (ch, m, 0))
    res = lambda shape: pl.BlockSpec(shape, lambda m: (0, 0))
    in_specs = [pl.BlockSpec((tm, ATT_WIDTH), lambda m: (m, 0)),
                pl.BlockSpec((tm, SSM_WIDTH), lambda m: (m, 0)),
                chunk(CH_MGA), chunk(CH_MGA + 1), chunk(CH_MGS), chunk(CH_MGS + 1),
                pl.BlockSpec((tm, d), lambda m: (m, 0)),
                pl.BlockSpec((None, 1, d), lambda m: (m // per_b, 0, 0)),
                res((ATT_WIDTH, d)), res((SSM_WIDTH, d)), res((d, d))]
    args = [o_att, y_ssm, p, p, p, p, x2, gate, w_att, w_ssm, w_out]
    if final:
        in_specs.append(res((1, d)))
        args.append(final_w.reshape(1, d))
    return pl.pallas_call(
        functools.partial(_merge_kernel, final=final),
        out_shape=jax.ShapeDtypeStruct((t, d), jnp.float32),
        grid=(t // tm,),
        in_specs=in_specs,
        out_specs=pl.BlockSpec((tm, d), lambda m: (m, 0)),
        compiler_params=_cparams(("arbitrary",)),
        name="merge_out",
    )(*args)


def _reorder_w_in(w_in_l):
    sizes = [ATT_WIDTH, ATT_WIDTH, ATT_WIDTH, ATT_WIDTH, SSM_WIDTH,
             SSM_WIDTH + 2 * SSM_GROUPS * SSM_STATE, SSM_HEADS, D_MODEL, D_MODEL]
    offs = np.cumsum([0] + sizes)
    seg = lambda i: w_in_l[:, offs[i]:offs[i + 1]]
    q, k, v, g, z, xbc, dtc, mga, mgs = [seg(i) for i in range(9)]
    w_main = jnp.concatenate([q, k, v, g, z, xbc, mga, mgs], axis=1).astype(jnp.bfloat16)
    w_dt = jnp.pad(dtc, ((0, 0), (0, LANES - SSM_HEADS))).astype(jnp.bfloat16)
    return w_main, w_dt


def _expand_matrix():
    e = np.zeros((LANES, SSM_WIDTH), np.float32)
    for j in range(SSM_HEADS):
        e[j, j * SSM_HEAD_DIM:(j + 1) * SSM_HEAD_DIM] = 1.0
    return jnp.asarray(np.concatenate([e, e], axis=0), dtype=jnp.bfloat16)


def _conv_shift_matrix():
    L = SSM_CHUNK
    s = np.zeros(((SSM_CONV - 1) * L, 2 * L), np.float32)
    for tap in range(SSM_CONV - 1):
        for l in range(L):
            s[tap * L + l, L + l - (SSM_CONV - 1 - tap)] = 1.0
    return jnp.asarray(s, dtype=jnp.bfloat16)


def kernel(x, c, positions, w_ada, b_ada, norm_w, w_in, lambda_q1, lambda_k1, lambda_q2, lambda_k2,
           attn_subln_w, w_att_branch, conv_w, conv_b, dt_bias, a_log, d_skip, ssm_norm_w,
           w_ssm_branch, w_out, final_norm_w):
    bsz, seq, d = x.shape
    depth = w_ada.shape[0]
    t = bsz * seq
    x2 = x.reshape(t, d)

    c_pad = jnp.pad(c, ((0, SUBLANES - bsz % SUBLANES), (0, 0))) if bsz % SUBLANES else c
    mod = _ada_mod(c_pad, w_ada, b_ada)[:, :bsz]
    rope_c, rope_sa, rope_sb = _rope_tables(positions)

    for l in range(depth):
        lambda_init = 0.8 - 0.6 * math.exp(-0.3 * l)
        shift = mod[l, :, :d].reshape(bsz, 1, d)
        scale = mod[l, :, d:2 * d].reshape(bsz, 1, d)
        gate = mod[l, :, 2 * d:].reshape(bsz, 1, d)
        w_main, w_dt = _reorder_w_in(w_in[l])
        p, dt_raw = _in_proj(x2, seq, norm_w[l], scale, shift, w_main, w_dt,
                             rope_c, rope_sa, rope_sb)
        lam_prm = jnp.zeros((SUBLANES, LANES), jnp.float32)
        lam_prm = lam_prm.at[0:4, :ATT_HEAD_DIM].set(
            jnp.stack([lambda_q1[l], lambda_k1[l], lambda_q2[l], lambda_k2[l]]))
        o_att = _diff_attention(p, bsz, seq, lam_prm, attn_subln_w[l], lambda_init)
        y_ssm = _ssd_branch(p, dt_raw, bsz, seq, conv_w[l], conv_b[l], dt_bias[l], a_log[l],
                            d_skip[l], ssm_norm_w[l])
        x2 = _merge_out(o_att, y_ssm, p, x2, seq, gate,
                        w_att_branch[l].astype(jnp.bfloat16), w_ssm_branch[l].astype(jnp.bfloat16),
                        w_out[l].astype(jnp.bfloat16),
                        final_norm_w if l == depth - 1 else None)
    return x2.reshape(bsz, seq, d)
```

```python
import functools
import math

import numpy as np
import jax
import jax.numpy as jnp
from jax import lax
from jax.experimental import pallas as pl
from jax.experimental.pallas import tpu as pltpu

D_MODEL = 1024
ATT_HEADS = 4
ATT_HEAD_DIM = 64
ATT_V_DIM = 2 * ATT_HEAD_DIM
ATT_WIDTH = ATT_HEADS * ATT_V_DIM
ROPE_THETA = 500000.0
ROPE_DIM = ATT_HEAD_DIM // 4
SSM_HEAD_DIM = 64
SSM_WIDTH = D_MODEL
SSM_HEADS = SSM_WIDTH // SSM_HEAD_DIM
SSM_GROUPS = 2
SSM_STATE = 128
SSM_CONV = 4
SSM_CHUNK = 128
GROUP_WIDTH = SSM_WIDTH // SSM_GROUPS
HEADS_PER_GROUP = SSM_HEADS // SSM_GROUPS
EPS = 1e-5

LANES = 128
SUBLANES = 8
CHUNK_COLS = 512
CH_Q, CH_K, CH_V, CH_G = 0, 1, 2, 3
CH_Z = 4
CH_XS = 6
CH_BC = 8
CH_MGA = 9
CH_MGS = 11
N_CHUNKS = 13
LOG2_E = math.log2(math.e)
NEG_BIG = -1e30

V7X_VMEM_BYTES = 64 * 1024 * 1024
VMEM_LIMIT = V7X_VMEM_BYTES * 3 // 4

TM_PROJ = 512
TQ = 1024
TK = 512
ATT_ROWS = 512
ATT_DIAG_ROWS = 256
TM_OUT = 512
SSD_SUB = 4


def _cparams(sem):
    return pltpu.CompilerParams(dimension_semantics=sem, vmem_limit_bytes=VMEM_LIMIT)


def _sigmoid(x):
    return 0.5 + 0.5 * jnp.tanh(0.5 * x)


def _silu(x):
    h = 0.5 * x
    return h + h * jnp.tanh(h)


def _split3(x):
    hi = x.astype(jnp.bfloat16)
    r1 = x - hi.astype(jnp.float32)
    mid = r1.astype(jnp.bfloat16)
    lo = (r1 - mid.astype(jnp.float32)).astype(jnp.bfloat16)
    return hi, mid, lo


def _ada_kernel(c_ref, w_ref, b_ref, o_ref):
    o_ref[...] = jnp.dot(c_ref[...], w_ref[...], preferred_element_type=jnp.float32,
                         precision=lax.Precision.HIGHEST) + b_ref[...]


def _ada_mod(c_pad, w_ada, b_ada):
    depth, d, d3 = w_ada.shape
    rows = c_pad.shape[0]
    nblk = d3 // d
    return pl.pallas_call(
        _ada_kernel,
        out_shape=jax.ShapeDtypeStruct((depth, rows, d3), jnp.float32),
        grid=(depth, nblk),
        in_specs=[pl.BlockSpec((rows, d), lambda l, n: (0, 0)),
                  pl.BlockSpec((None, d, d), lambda l, n: (l, 0, n)),
                  pl.BlockSpec((None, 1, d), lambda l, n: (l, 0, n))],
        out_specs=pl.BlockSpec((None, rows, d), lambda l, n: (l, 0, n)),
        compiler_params=_cparams(("arbitrary", "arbitrary")),
        name="ada_mod",
    )(c_pad, w_ada, b_ada.reshape(depth, 1, d3))


def _rope_kernel(pos_ref, freq_ref, ma_ref, mb_ref, c_ref, sa_ref, sb_ref):
    ang = pos_ref[...].astype(jnp.float32) * freq_ref[...]
    sn = jnp.sin(ang)
    c_ref[...] = jnp.cos(ang)
    sa_ref[...] = sn * ma_ref[...]
    sb_ref[...] = sn * mb_ref[...]


def _rope_tables(positions):
    t = positions.size
    half = ROPE_DIM // 2
    inv_freq = ROPE_THETA ** (-jnp.arange(0, ROPE_DIM, 2, dtype=jnp.float32) / ROPE_DIM)
    d = np.arange(LANES) % ATT_HEAD_DIM
    in_rope = d < ROPE_DIM
    freq = jnp.where(in_rope, inv_freq[d % half], 0.0).astype(jnp.float32).reshape(1, LANES)
    ma = jnp.asarray((in_rope & (d >= half)).astype(np.float32)).reshape(1, LANES)
    mb = jnp.asarray(-(d < half).astype(np.float32)).reshape(1, LANES)
    pos_b = jnp.broadcast_to(positions.reshape(t, 1), (t, LANES))
    tr = 1024
    row = pl.BlockSpec((1, LANES), lambda i: (0, 0))
    tab = pl.BlockSpec((tr, LANES), lambda i: (i, 0))
    sds = jax.ShapeDtypeStruct((t, LANES), jnp.float32)
    return pl.pallas_call(
        _rope_kernel, out_shape=(sds, sds, sds), grid=(t // tr,),
        in_specs=[tab, row, row, row], out_specs=(tab, tab, tab),
        compiler_params=_cparams(("arbitrary",)), name="rope_tables",
    )(pos_b, freq, ma, mb)


def _inproj_kernel(x_ref, nw_ref, sc_ref, sh_ref, w_ref, wdt_ref, c_ref, sa_ref, sb_ref,
                   p_ref, dt_ref, h_scr):
    x = x_ref[...]
    y = x * lax.rsqrt(jnp.mean(x * x, axis=-1, keepdims=True) + EPS) * nw_ref[...]
    h = y * (1.0 + sc_ref[...]) + sh_ref[...]
    h_scr[...] = h.astype(jnp.bfloat16)
    dt_ref[...] = jnp.dot(h_scr[...], wdt_ref[...], preferred_element_type=jnp.float32)

    def rope(t):
        reps = CHUNK_COLS // LANES
        c = jnp.tile(c_ref[...], (1, reps))
        sa = jnp.tile(sa_ref[...], (1, reps))
        sb = jnp.tile(sb_ref[...], (1, reps))
        half = ROPE_DIM // 2
        return (t * c + pltpu.roll(t, half, 1) * sa + pltpu.roll(t, CHUNK_COLS - half, 1) * sb)

    for n in range(N_CHUNKS):
        acc = jnp.dot(h_scr[...], w_ref[:, n * CHUNK_COLS:(n + 1) * CHUNK_COLS],
                      preferred_element_type=jnp.float32)
        if n == CH_Q:
            acc = rope(acc) * (ATT_HEAD_DIM ** -0.5 * LOG2_E)
        elif n == CH_K:
            acc = rope(acc)
        p_ref[n] = acc.astype(p_ref.dtype)


def _in_proj(x2, seq, norm_w, scale, shift, w_main, w_dt, rope_c, rope_sa, rope_sb):
    t, d = x2.shape
    tm = min(TM_PROJ, seq)
    per_b = seq // tm
    vec = lambda: pl.BlockSpec((None, 1, d), lambda m: (m // per_b, 0, 0))
    tab = lambda: pl.BlockSpec((tm, LANES), lambda m: (m, 0))
    once = lambda shape: pl.BlockSpec(shape, lambda m: (0, 0), pipeline_mode=pl.Buffered(1))
    return pl.pallas_call(
        _inproj_kernel,
        out_shape=(jax.ShapeDtypeStruct((N_CHUNKS, t, CHUNK_COLS), jnp.bfloat16),
                   jax.ShapeDtypeStruct((t, LANES), jnp.float32)),
        grid=(t // tm,),
        in_specs=[pl.BlockSpec((tm, d), lambda m: (m, 0)),
                  once((1, d)), vec(), vec(),
                  once((d, N_CHUNKS * CHUNK_COLS)), once((d, LANES)),
                  tab(), tab(), tab()],
        out_specs=(pl.BlockSpec((N_CHUNKS, tm, CHUNK_COLS), lambda m: (0, m, 0)),
                   pl.BlockSpec((tm, LANES), lambda m: (m, 0))),
        scratch_shapes=[pltpu.VMEM((tm, d), jnp.bfloat16)],
        compiler_params=_cparams(("arbitrary",)),
        name="in_proj",
    )(x2, norm_w.reshape(1, d), scale, shift, w_main, w_dt, rope_c, rope_sa, rope_sb)


def _attn_kernel(lam_ref, q_ref, k_ref, v_ref, g_ref, w_ref, o_ref, qs_scr, vx_scr, sa_scr, sb_scr,
                 m_scr, acc_scr, *, tq, lambda_init):
    qi = pl.program_id(2)

    @pl.when(qi == 0)
    def _():
        vx_scr[:, :LANES] = v_ref[...]
        vx_scr[:, LANES:] = jnp.ones((vx_scr.shape[0], LANES), vx_scr.dtype)

    q = q_ref[...]
    lane = lax.broadcasted_iota(jnp.int32, q.shape, 1)
    zero = jnp.zeros_like(q)
    qs_scr[0:tq, :] = jnp.where(lane < ATT_HEAD_DIM, q, zero)
    qs_scr[tq:, :] = jnp.where(lane >= ATT_HEAD_DIM, q, zero)
    m_scr[...] = jnp.full(m_scr.shape, NEG_BIG, jnp.float32)
    acc_scr[...] = jnp.zeros(acc_scr.shape, jnp.float32)

    tk = min(TK, tq // 2)
    n_diag = tq // tk
    assert tq % (2 * tk) == 0

    def chains(diag):
        height = ATT_ROWS if diag is None else ATT_DIAG_ROWS
        out = []
        for rc in range(2 * tq // height):
            rows = slice(rc * height, (rc + 1) * height)
            if diag is None:
                out.append((rows, tk, None))
                continue
            rel0 = (rc * height) % tq - diag * tk
            if rel0 >= tk:
                out.append((rows, tk, None))
            elif rel0 + height > 0:
                out.append((rows, min(tk, rel0 + height), rel0))
        return out

    def scores(j, dst, diag=None):
        k = k_ref[pl.ds(pl.multiple_of(j * tk, tk), tk), :]
        for rows, _, _ in chains(diag):
            dst[rows, :] = lax.dot_general(qs_scr[rows, :], k, (((1,), (1,)), ((), ())),
                                           preferred_element_type=jnp.float32)

    def consume(j, src, diag=None):
        start = pl.multiple_of(j * tk, tk)
        for rows, ncol, rel0 in chains(diag):
            s = src[rows, :ncol]
            if rel0 is not None:
                r = lax.broadcasted_iota(jnp.int32, s.shape, 0) + rel0
                c = lax.broadcasted_iota(jnp.int32, s.shape, 1)
                s = jnp.where(c <= r, s, NEG_BIG)
            m_prev = m_scr[rows, :]
            m_new = jnp.maximum(m_prev, jnp.max(s, axis=-1, keepdims=True))
            alpha = jnp.exp2(m_prev - m_new)
            p = jnp.exp2(s - jnp.tile(m_new, (1, ncol // LANES)))
            pv = jnp.dot(p.astype(jnp.bfloat16), vx_scr[pl.ds(start, ncol), :],
                         preferred_element_type=jnp.float32)
            acc_scr[rows, :] = jnp.tile(alpha, (1, 2)) * acc_scr[rows, :] + pv
            m_scr[rows, :] = m_new

    scores(0, sa_scr)

    def pair(j):
        scores(j + 1, sb_scr)
        consume(j, sa_scr)
        scores(j + 2, sa_scr)
        consume(j + 1, sb_scr)

    def body(i, carry):
        pair(4 * i)
        pair(4 * i + 2)
        return carry

    n_full = n_diag * qi
    n_quads = n_full // 4
    lax.fori_loop(0, n_quads, body, 0)

    @pl.when(n_full % 4 == 2)
    def _():
        pair(4 * n_quads)

    bufs = (sa_scr, sb_scr)
    for d in range(n_diag):
        if d + 1 < n_diag:
            scores(n_full + d + 1, bufs[(d + 1) % 2], diag=d + 1)
        consume(n_full + d, bufs[d % 2], diag=d)

    prm = lam_ref[...]
    lam = (jnp.exp(jnp.sum(prm[0:1] * prm[1:2], axis=-1, keepdims=True))
           - jnp.exp(jnp.sum(prm[2:3] * prm[3:4], axis=-1, keepdims=True)) + lambda_init)
    o_all = acc_scr[:, :LANES] / acc_scr[:, LANES:]
    o = o_all[:tq] - lam * o_all[tq:]
    o = o * lax.rsqrt(jnp.mean(o * o, axis=-1, keepdims=True) + EPS) * w_ref[...]
    o = o * (1.0 - lambda_init)
    o_ref[...] = (o * _silu(g_ref[...].astype(jnp.float32))).astype(o_ref.dtype)


def _diff_attention(p, bsz, seq, lam_prm, subln_w, lambda_init):
    t = bsz * seq
    tq = min(TQ, seq)
    tk = min(TK, tq // 2)
    nq = seq // tq
    blk = lambda ch: pl.BlockSpec((None, tq, LANES), lambda b, h, i: (ch, b * nq + i, h))
    full = lambda ch: pl.BlockSpec((None, seq, LANES), lambda b, h, i: (ch, b, h))
    return pl.pallas_call(
        functools.partial(_attn_kernel, tq=tq, lambda_init=lambda_init),
        out_shape=jax.ShapeDtypeStruct((t, ATT_WIDTH), jnp.bfloat16),
        grid=(bsz, ATT_HEADS, nq),
        in_specs=[pl.BlockSpec((SUBLANES, LANES), lambda b, h, i: (0, 0)),
                  blk(CH_Q), full(CH_K), full(CH_V), blk(CH_G),
                  pl.BlockSpec((1, LANES), lambda b, h, i: (0, 0))],
        out_specs=pl.BlockSpec((tq, LANES), lambda b, h, i: (b * nq + i, h)),
        scratch_shapes=[pltpu.VMEM((2 * tq, LANES), jnp.bfloat16),
                        pltpu.VMEM((seq, 2 * LANES), jnp.bfloat16),
                        pltpu.VMEM((2 * tq, tk), jnp.float32),
                        pltpu.VMEM((2 * tq, tk), jnp.float32),
                        pltpu.VMEM((2 * tq, LANES), jnp.float32),
                        pltpu.VMEM((2 * tq, 2 * LANES), jnp.float32)],
        compiler_params=_cparams(("arbitrary", "arbitrary", "arbitrary")),
        name="diff_attn",
    )(lam_prm, p, p, p, p, subln_w.reshape(1, LANES))


def _ssd_kernel(xs0_ref, xs1_ref, bc_ref, z0_ref, z1_ref, dt_ref, cw_ref, cb_ref, dtb_ref,
                alog_ref, dsk_ref, nw_ref, e2_ref, shift_ref, o_ref, carry_scr, h_scr):
    L = SSM_CHUNK
    conv_ch = 3 * CHUNK_COLS
    step = pl.program_id(1)
    parity = step % 2

    @pl.when(step == 0)
    def _():
        carry_scr[1] = jnp.zeros((L, conv_ch), carry_scr.dtype)
        h_scr[...] = jnp.zeros(h_scr.shape, jnp.float32)

    r_i = lax.broadcasted_iota(jnp.int32, (L, L), 0)
    c_i = lax.broadcasted_iota(jnp.int32, (L, L), 1)
    causal = c_i <= r_i
    tril = jnp.where(causal, 1.0, 0.0).astype(jnp.bfloat16)
    tril3 = jnp.concatenate([tril, tril, tril], axis=1)
    lane = lax.broadcasted_iota(jnp.int32, (L, LANES), 1)
    neg_a = -jnp.exp(alog_ref[...])

    prev = carry_scr[1 - parity]
    for sub in range(SSD_SUB):
        rows = slice(sub * L, (sub + 1) * L)

        cur = jnp.concatenate([xs0_ref[rows, :], xs1_ref[rows, :], bc_ref[rows, :]], axis=1)
        shifted = jnp.dot(shift_ref[...], jnp.concatenate([prev, cur], axis=0),
                          preferred_element_type=jnp.float32)
        conv = cb_ref[...] + cur.astype(jnp.float32) * cw_ref[SSM_CONV - 1:SSM_CONV, :]
        for tap in range(SSM_CONV - 1):
            conv = conv + shifted[tap * L:(tap + 1) * L, :] * cw_ref[tap:tap + 1, :]
        prev = cur
        act = _silu(conv)
        xs = act[:, :SSM_WIDTH]
        b_mat = act[:, SSM_WIDTH:SSM_WIDTH + SSM_GROUPS * SSM_STATE]
        c_mat = act[:, SSM_WIDTH + SSM_GROUPS * SSM_STATE:]

        dtr = dt_ref[rows, :] + dtb_ref[...]
        dt = jnp.maximum(dtr, 0.0) + jnp.log1p(jnp.exp(-jnp.abs(dtr)))
        hi, mid, lo = _split3(dt * neg_a)
        acs = jnp.dot(tril3, jnp.concatenate([hi, mid, lo], axis=0),
                      preferred_element_type=jnp.float32) * LOG2_E
        acs_t = acs.T

        both = jnp.concatenate([dt, acs], axis=0)
        hi = both.astype(jnp.bfloat16)
        lo = (both - hi.astype(jnp.float32)).astype(jnp.bfloat16)
        ex = jnp.dot(jnp.concatenate([hi, lo], axis=1), e2_ref[...],
                     preferred_element_type=jnp.float32)
        dt_e = ex[:L]
        acs_e = ex[L:]
        acs_last = acs_e[L - 1:L, :]
        eacs_e = jnp.exp2(acs_e)
        xd = xs * dt_e
        xdd = (xd * jnp.exp2(acs_last - acs_e)).astype(jnp.bfloat16)
        chunk_decay = jnp.exp2(acs_last)

        b_t = b_mat.T
        for g in range(SSM_GROUPS):
            gs = slice(g * GROUP_WIDTH, (g + 1) * GROUP_WIDTH)
            c_g = c_mat[:, g * SSM_STATE:(g + 1) * SSM_STATE].astype(jnp.bfloat16)
            bt_g = b_t[g * SSM_STATE:(g + 1) * SSM_STATE, :].astype(jnp.bfloat16)
            cb = jnp.dot(c_g, bt_g, preferred_element_type=jnp.float32)

            h_g = h_scr[:, gs]
            y_off = jnp.dot(c_g, h_g.astype(jnp.bfloat16),
                            preferred_element_type=jnp.float32) * eacs_e[:, gs]
            st = jnp.dot(bt_g, xdd[:, gs], preferred_element_type=jnp.float32)
            h_scr[:, gs] = h_g * chunk_decay[:, gs] + st

            parts = []
            for pr in range(HEADS_PER_GROUP // 2):
                col0 = g * GROUP_WIDTH + pr * LANES
                xd_pair = xd[:, col0:col0 + LANES]
                ms = []
                for hh in range(2):
                    j = g * HEADS_PER_GROUP + 2 * pr + hh
                    seg = acs[:, j:j + 1] - acs_t[j:j + 1, :]
                    lm = jnp.exp2(jnp.where(causal, seg, NEG_BIG))
                    ms.append((cb * lm).astype(jnp.bfloat16))
                w_pair = jnp.concatenate(
                    [jnp.where(lane < SSM_HEAD_DIM, xd_pair, 0.0),
                     jnp.where(lane >= SSM_HEAD_DIM, xd_pair, 0.0)], axis=0).astype(jnp.bfloat16)
                parts.append(jnp.dot(jnp.concatenate(ms, axis=1), w_pair,
                                     preferred_element_type=jnp.float32))
            y_diag = jnp.concatenate(parts, axis=1)

            y = y_diag + y_off + dsk_ref[:, gs] * xs[:, gs]
            zg = (z0_ref if g == 0 else z1_ref)[rows, :].astype(jnp.float32)
            y = y * _silu(zg)
            y = y * lax.rsqrt(jnp.mean(y * y, axis=-1, keepdims=True) + EPS) * nw_ref[:, gs]
            o_ref[rows, gs] = y.astype(o_ref.dtype)
    carry_scr[parity] = prev


def _ssd_branch(p, dt_raw, bsz, seq, conv_w, conv_b, dt_bias, a_log, d_skip, ssm_norm_w):
    t = bsz * seq
    L = SSM_CHUNK
    rows = SSD_SUB * L
    ns = seq // rows
    conv_ch = 3 * CHUNK_COLS
    blk = lambda ch: pl.BlockSpec((None, rows, CHUNK_COLS), lambda b, c: (ch, b * ns + c, 0))
    row = lambda w: pl.BlockSpec((1, w), lambda b, c: (0, 0))
    pad16 = lambda v: jnp.pad(v.astype(jnp.float32), (0, LANES - SSM_HEADS)).reshape(1, LANES)
    return pl.pallas_call(
        _ssd_kernel,
        out_shape=jax.ShapeDtypeStruct((t, SSM_WIDTH), jnp.bfloat16),
        grid=(bsz, ns),
        in_specs=[blk(CH_XS), blk(CH_XS + 1), blk(CH_BC), blk(CH_Z), blk(CH_Z + 1),
                  pl.BlockSpec((rows, LANES), lambda b, c: (b * ns + c, 0)),
                  pl.BlockSpec((SSM_CONV, conv_ch), lambda b, c: (0, 0)),
                  row(conv_ch), row(LANES), row(LANES), row(SSM_WIDTH), row(SSM_WIDTH),
                  pl.BlockSpec((2 * LANES, SSM_WIDTH), lambda b, c: (0, 0)),
                  pl.BlockSpec(((SSM_CONV - 1) * L, 2 * L), lambda b, c: (0, 0))],
        out_specs=pl.BlockSpec((rows, SSM_WIDTH), lambda b, c: (b * ns + c, 0)),
        scratch_shapes=[pltpu.VMEM((2, L, conv_ch), jnp.bfloat16),
                        pltpu.VMEM((SSM_STATE, SSM_WIDTH), jnp.float32)],
        compiler_params=_cparams(("arbitrary", "arbitrary")),
        name="ssd_branch",
    )(p, p, p, p, p, dt_raw, conv_w, conv_b.reshape(1, conv_ch), pad16(dt_bias), pad16(a_log),
      jnp.repeat(d_skip.astype(jnp.float32), SSM_HEAD_DIM).reshape(1, SSM_WIDTH),
      ssm_norm_w.reshape(1, SSM_WIDTH), _expand_matrix(), _conv_shift_matrix())


def _merge_kernel(*refs, final):
    if final:
        (oa_ref, ys_ref, ga0_ref, ga1_ref, gs0_ref, gs1_ref, x_ref, gate_ref,
         wa_ref, ws_ref, wo_ref, fw_ref, o_ref) = refs
    else:
        (oa_ref, ys_ref, ga0_ref, ga1_ref, gs0_ref, gs1_ref, x_ref, gate_ref,
         wa_ref, ws_ref, wo_ref, o_ref) = refs
    y_att = jnp.dot(oa_ref[...], wa_ref[...], preferred_element_type=jnp.float32)
    y_ssm = jnp.dot(ys_ref[...], ws_ref[...], preferred_element_type=jnp.float32)
    mg_att = jnp.concatenate([ga0_ref[...], ga1_ref[...]], axis=1).astype(jnp.float32)
    mg_ssm = jnp.concatenate([gs0_ref[...], gs1_ref[...]], axis=1).astype(jnp.float32)
    merged = _sigmoid(mg_att) * y_att + _sigmoid(mg_ssm) * y_ssm
    out = x_ref[...] + gate_ref[...] * jnp.dot(merged.astype(jnp.bfloat16), wo_ref[...],
                                                preferred_element_type=jnp.float32)
    if final:
        out = out * lax.rsqrt(jnp.mean(out * out, axis=-1, keepdims=True) + EPS) * fw_ref[...]
    o_ref[...] = out


def _merge_out(o_att, y_ssm, p, x2, seq, gate, w_att, w_ssm, w_out, final_w):
    t, d = x2.shape
    tm = min(TM_OUT, seq)
    per_b = seq // tm
    final = final_w is not None
    chunk = lambda ch: pl.BlockSpec((None, tm, CHUNK_COLS), lambda m: (ch, m, 0))
    res = lambda shape: pl.BlockSpec(shape, lambda m: (0, 0))
    in_specs = [pl.BlockSpec((tm, ATT_WIDTH), lambda m: (m, 0)),
                pl.BlockSpec((tm, SSM_WIDTH), lambda m: (m, 0)),
                chunk(CH_MGA), chunk(CH_MGA + 1), chunk(CH_MGS), chunk(CH_MGS + 1),
                pl.BlockSpec((tm, d), lambda m: (m, 0)),
                pl.BlockSpec((None, 1, d), lambda m: (m // per_b, 0, 0)),
                res((ATT_WIDTH, d)), res((SSM_WIDTH, d)), res((d, d))]
    args = [o_att, y_ssm, p, p, p, p, x2, gate, w_att, w_ssm, w_out]
    if final:
        in_specs.append(res((1, d)))
        args.append(final_w.reshape(1, d))
    return pl.pallas_call(
        functools.partial(_merge_kernel, final=final),
        out_shape=jax.ShapeDtypeStruct((t, d), jnp.float32),
        grid=(t // tm,),
        in_specs=in_specs,
        out_specs=pl.BlockSpec((tm, d), lambda m: (m, 0)),
        compiler_params=_cparams(("arbitrary",)),
        name="merge_out",
    )(*args)


def _reorder_w_in(w_in_l):
    sizes = [ATT_WIDTH, ATT_WIDTH, ATT_WIDTH, ATT_WIDTH, SSM_WIDTH,
             SSM_WIDTH + 2 * SSM_GROUPS * SSM_STATE, SSM_HEADS, D_MODEL, D_MODEL]
    offs = np.cumsum([0] + sizes)
    seg = lambda i: w_in_l[:, offs[i]:offs[i + 1]]
    q, k, v, g, z, xbc, dtc, mga, mgs = [seg(i) for i in range(9)]
    w_main = jnp.concatenate([q, k, v, g, z, xbc, mga, mgs], axis=1).astype(jnp.bfloat16)
    w_dt = jnp.pad(dtc, ((0, 0), (0, LANES - SSM_HEADS))).astype(jnp.bfloat16)
    return w_main, w_dt


def _expand_matrix():
    e = np.zeros((LANES, SSM_WIDTH), np.float32)
    for j in range(SSM_HEADS):
        e[j, j * SSM_HEAD_DIM:(j + 1) * SSM_HEAD_DIM] = 1.0
    return jnp.asarray(np.concatenate([e, e], axis=0), dtype=jnp.bfloat16)


def _conv_shift_matrix():
    L = SSM_CHUNK
    s = np.zeros(((SSM_CONV - 1) * L, 2 * L), np.float32)
    for tap in range(SSM_CONV - 1):
        for l in range(L):
            s[tap * L + l, L + l - (SSM_CONV - 1 - tap)] = 1.0
    return jnp.asarray(s, dtype=jnp.bfloat16)


def kernel(x, c, positions, w_ada, b_ada, norm_w, w_in, lambda_q1, lambda_k1, lambda_q2, lambda_k2,
           attn_subln_w, w_att_branch, conv_w, conv_b, dt_bias, a_log, d_skip, ssm_norm_w,
           w_ssm_branch, w_out, final_norm_w):
    bsz, seq, d = x.shape
    depth = w_ada.shape[0]
    t = bsz * seq
    x2 = x.reshape(t, d)

    c_pad = jnp.pad(c, ((0, SUBLANES - bsz % SUBLANES), (0, 0))) if bsz % SUBLANES else c
    mod = _ada_mod(c_pad, w_ada, b_ada)[:, :bsz]
    rope_c, rope_sa, rope_sb = _rope_tables(positions)

    for l in range(depth):
        lambda_init = 0.8 - 0.6 * math.exp(-0.3 * l)
        shift = mod[l, :, :d].reshape(bsz, 1, d)
        scale = mod[l, :, d:2 * d].reshape(bsz, 1, d)
        gate = mod[l, :, 2 * d:].reshape(bsz, 1, d)
        w_main, w_dt = _reorder_w_in(w_in[l])
        p, dt_raw = _in_proj(x2, seq, norm_w[l], scale, shift, w_main, w_dt,
                             rope_c, rope_sa, rope_sb)
        lam_prm = jnp.zeros((SUBLANES, LANES), jnp.float32)
        lam_prm = lam_prm.at[0:4, :ATT_HEAD_DIM].set(
            jnp.stack([lambda_q1[l], lambda_k1[l], lambda_q2[l], lambda_k2[l]]))
        o_att = _diff_attention(p, bsz, seq, lam_prm, attn_subln_w[l], lambda_init)
        y_ssm = _ssd_branch(p, dt_raw, bsz, seq, conv_w[l], conv_b[l], dt_bias[l], a_log[l],
                            d_skip[l], ssm_norm_w[l])
        x2 = _merge_out(o_att, y_ssm, p, x2, seq, gate,
                        w_att_branch[l].astype(jnp.bfloat16), w_ssm_branch[l].astype(jnp.bfloat16),
                        w_out[l].astype(jnp.bfloat16),
                        final_norm_w if l == depth - 1 else None)
    return x2.reshape(bsz, seq, d)
```
